```python
import jax, jax.numpy as jnp
from jax import lax
import numpy as np

D_MODEL = 1024
BATCH = 16
SEQ = 4096
DEPTH = 2
DEC_BATCH = 16
DEC_SEQ = 64
PAST_LEN = 2048

CHUNK = 64
EPS = 1e-6
MIN_FORGET = 1e-20
D_LRU = D_MODEL
N_LRU_BLOCKS = 8
LRU_BLOCK = D_LRU // N_LRU_BLOCKS
CONV_W = 4
LRU_C = 8.0
D_HGRN = D_MODEL
HGRN_HEAD_DIM = 128
N_HGRN_HEADS = D_HGRN // HGRN_HEAD_DIM
D_ATT = D_MODEL
ATT_HEAD_DIM = 64
N_ATT_HEADS = D_ATT // ATT_HEAD_DIM
ATT_SCALE = ATT_HEAD_DIM ** -0.5
PAST_CHUNKS = 8
BAND_PAST = PAST_CHUNKS * CHUNK
BAND_LEN = BAND_PAST + CHUNK
MAX_REL = 128
N_REL = 2 * MAX_REL + 1
MASK_VALUE = -1e30
D_FF = 2816
N_EXPERTS = 8
TOP_K = 2
D_FF_EXPERT = 2816
N_DENSE = (DEPTH + 1) // 2
N_MOE = DEPTH // 2
IN_WIDTHS = (D_LRU, D_LRU, D_HGRN, D_HGRN, D_HGRN, D_HGRN, D_ATT, D_ATT, D_ATT, D_MODEL, D_MODEL, D_MODEL)
D_IN = sum(IN_WIDTHS)
IN_SPLIT_POINTS = tuple(int(v) for v in np.cumsum(IN_WIDTHS)[:-1])

kernel_name = "hybrid_streaming_encoder_step"


def rms_norm(x, g):
    xf = x.astype(jnp.float32)
    y = xf * lax.rsqrt(jnp.mean(xf * xf, axis=-1, keepdims=True) + EPS)
    return (y * g.astype(jnp.float32)).astype(x.dtype)


def causal_dwconv(x, buf, w, b):
    L = x.shape[1]
    xp = jnp.concatenate([buf.astype(x.dtype), x], axis=1)
    y = b
    for j in range(CONV_W):
        y = y + w[j] * xp[:, j:j + L]
    return y, xp[:, -(CONV_W - 1):]


def rg_lru(x, h0, w_r, b_r, w_i, b_i, lam):
    B, L, _ = x.shape
    xb = x.reshape(B, L, N_LRU_BLOCKS, LRU_BLOCK)
    r = jax.nn.sigmoid((jnp.einsum('blnc,ncd->blnd', xb, w_r).reshape(B, L, D_LRU) + b_r).astype(jnp.float32))
    i = jax.nn.sigmoid((jnp.einsum('blnc,ncd->blnd', xb, w_i).reshape(B, L, D_LRU) + b_i).astype(jnp.float32))
    log_a = -LRU_C * r * jax.nn.softplus(-lam.astype(jnp.float32))
    a = jnp.exp(log_a)
    u = jnp.sqrt(jnp.maximum(-jnp.expm1(2.0 * log_a), 0.0)) * i * x.astype(jnp.float32)

    def combine(left, right):
        a1, b1 = left
        a2, b2 = right
        return a1 * a2, a2 * b1 + b2

    a_cum, b_cum = lax.associative_scan(combine, (a, u), axis=1)
    h = a_cum * h0.astype(jnp.float32)[:, None] + b_cum
    return h.astype(x.dtype), h[:, -1]


def hgrn_lower_bounds(gamma):
    p = jax.nn.softmax(gamma.astype(jnp.float32), axis=0)
    return jnp.cumsum(p, axis=0) - p


def hgrn_chunk(S, xs):
    q, k, logf, v = xs
    L = q.shape[1]
    b = jnp.cumsum(logf, axis=1)
    inter = jnp.einsum('blhk,bhkv->blhv', q * jnp.exp(b), S)
    causal = jnp.tril(jnp.ones((L, L), dtype=bool))[None, :, :, None, None]
    diff = jnp.where(causal, b[:, :, None] - b[:, None, :], 0.0)
    decay = jnp.where(causal, jnp.exp(diff), 0.0)
    att = jnp.einsum('bthk,bshk,btshk->bhts', q, k, decay)
    intra = jnp.einsum('bhts,bshv->bthv', att, v)
    b_last = b[:, -1]
    S_new = jnp.exp(b_last)[..., None] * S + jnp.einsum('blhk,blhv->bhkv', k * jnp.exp(b_last[:, None] - b), v)
    return S_new, inter + intra


def hgrn_recurrence(S0, q, k, logf, v):
    B, L = q.shape[:2]
    n = max(L // CHUNK, 1)
    cl = L // n

    def to_chunks(t):
        return jnp.moveaxis(t.reshape(B, n, cl, *t.shape[2:]), 1, 0)

    S_last, o = lax.scan(hgrn_chunk, S0.astype(jnp.float32),
                         (to_chunks(q), to_chunks(k), to_chunks(logf), to_chunks(v)))
    o = jnp.moveaxis(o, 0, 1).reshape(B, L, N_HGRN_HEADS, HGRN_HEAD_DIM)
    return o, S_last


def rel_bias(table, dist):
    return table[:, jnp.clip(dist, -MAX_REL, MAX_REL) + MAX_REL]


def attend(q, k, v, bias, valid):
    s = jnp.einsum('bqhd,bkhd->bhqk', q, k).astype(jnp.float32) * ATT_SCALE + bias.astype(jnp.float32)[None]
    s = jnp.where(valid, s, MASK_VALUE)
    p = jax.nn.softmax(s, axis=-1).astype(v.dtype)
    return jnp.einsum('bhqk,bkhd->bqhd', p, v)


def band_attention_prompt(q, k, v, table):
    B, S, H, Dh = q.shape
    nc = S // CHUNK
    kp = jnp.pad(k, ((0, 0), (BAND_PAST, 0), (0, 0), (0, 0)))
    vp = jnp.pad(v, ((0, 0), (BAND_PAST, 0), (0, 0), (0, 0)))
    t = jnp.arange(CHUNK)
    j = jnp.arange(BAND_LEN)
    bias = rel_bias(table, t[:, None] + BAND_PAST - j[None, :])
    qc = jnp.moveaxis(q.reshape(B, nc, CHUNK, H, Dh), 1, 0)

    def one_chunk(args):
        c, qb = args
        kb = lax.dynamic_slice_in_dim(kp, c * CHUNK, BAND_LEN, axis=1)
        vb = lax.dynamic_slice_in_dim(vp, c * CHUNK, BAND_LEN, axis=1)
        valid = j >= BAND_PAST - c * CHUNK
        return attend(qb, kb, vb, bias, valid)

    out = lax.map(one_chunk, (jnp.arange(nc), qc))
    return jnp.moveaxis(out, 0, 1).reshape(B, S, H * Dh)


def band_attention_sample(q, k, v, cache_k, cache_v, table):
    B, L, H, Dh = q.shape
    CL = cache_k.shape[1]
    k_all = jnp.concatenate([cache_k.astype(k.dtype), k], axis=1)
    v_all = jnp.concatenate([cache_v.astype(v.dtype), v], axis=1)
    qpos = PAST_LEN + jnp.arange(L)
    kpos = jnp.concatenate([PAST_LEN - CL + jnp.arange(CL), PAST_LEN + jnp.arange(L)])
    bias = rel_bias(table, qpos[:, None] - kpos[None, :])
    valid = jnp.ones((CL + L,), dtype=bool)
    return attend(q, k_all, v_all, bias, valid).reshape(B, L, H * Dh)


def mixer(xn, w_in, conv_w, conv_b, lru_wr, lru_br, lru_wi, lru_bi, lru_lambda, lb, hgrn_norm,
          q_norm, k_norm, rel_table, w_out, conv_buf, h_lru, s_hgrn, cache_k, cache_v):
    B, L, _ = xn.shape
    proj = xn @ w_in
    a_x, a_g, b_q, b_f, b_v, b_g, c_q, c_k, c_v, g_a, g_b, g_c = jnp.split(proj, IN_SPLIT_POINTS, axis=-1)

    a_conv, new_buf = causal_dwconv(a_x, conv_buf, conv_w, conv_b)
    h_seq, h_last = rg_lru(a_conv, h_lru, lru_wr, lru_br, lru_wi, lru_bi, lru_lambda)
    y_a = h_seq * jax.nn.gelu(a_g)

    heads = lambda t: t.reshape(B, L, N_HGRN_HEADS, HGRN_HEAD_DIM).astype(jnp.float32)
    z = heads(b_f)
    lbh = lb.reshape(N_HGRN_HEADS, HGRN_HEAD_DIM).astype(jnp.float32)
    f = lbh + (1.0 - lbh) * jax.nn.sigmoid(z)
    logf = jnp.log(jnp.maximum(f, MIN_FORGET))
    k_b = (1.0 - lbh) * jax.nn.sigmoid(-z)
    o_b, s_new = hgrn_recurrence(s_hgrn, jax.nn.silu(heads(b_q)), k_b, logf, heads(b_v))
    o_b = rms_norm(o_b, hgrn_norm.reshape(N_HGRN_HEADS, HGRN_HEAD_DIM)).reshape(B, L, D_HGRN)
    y_b = o_b.astype(xn.dtype) * jax.nn.silu(b_g)

    q = rms_norm(c_q.reshape(B, L, N_ATT_HEADS, ATT_HEAD_DIM), q_norm)
    k = rms_norm(c_k.reshape(B, L, N_ATT_HEADS, ATT_HEAD_DIM), k_norm)
    v = c_v.reshape(B, L, N_ATT_HEADS, ATT_HEAD_DIM)
    if cache_k is None:
        y_c = band_attention_prompt(q, k, v, rel_table)
        keep = min(BAND_PAST, L)
        new_k, new_v = k[:, L - keep:], v[:, L - keep:]
    else:
        y_c = band_attention_sample(q, k, v, cache_k, cache_v, rel_table)
        new_k, new_v = k, v

    merged = jax.nn.sigmoid(g_a) * y_a + jax.nn.sigmoid(g_b) * y_b + jax.nn.sigmoid(g_c) * y_c
    states = (new_buf.astype(xn.dtype), h_last.astype(xn.dtype), s_new.astype(xn.dtype), new_k, new_v)
    return merged @ w_out, states


def swiglu(x, w_gate, w_up, w_down):
    return (jax.nn.silu(x @ w_gate) * (x @ w_up)) @ w_down


def moe_swiglu(x, w_router, w_gate, w_up, w_down):
    logits = (x @ w_router).astype(jnp.float32)
    top_v, top_i = lax.top_k(logits, TOP_K)
    top_p = jax.nn.softmax(top_v, axis=-1)
    gates = jnp.sum(jax.nn.one_hot(top_i, N_EXPERTS, dtype=jnp.float32) * top_p[..., None], axis=-2).astype(x.dtype)
    y = jnp.zeros_like(x)
    for e in range(N_EXPERTS):
        y = y + gates[..., e:e + 1] * swiglu(x, w_gate[e], w_up[e], w_down[e])
    return y


def channel_mixer(l, x, ffn_w_gate, ffn_w_up, ffn_w_down, moe_router, moe_w_gate, moe_w_up, moe_w_down):
    if l % 2 == 0:
        i = l // 2
        return swiglu(x, ffn_w_gate[i], ffn_w_up[i], ffn_w_down[i])
    i = l // 2
    return moe_swiglu(x, moe_router[i], moe_w_gate[i], moe_w_up[i], moe_w_down[i])


def setup_inputs(seed: int = 0) -> dict:
    key = jax.random.key(seed)
    ks = iter(jax.random.split(key, 40))
    nrm = lambda shape, s: jax.random.normal(next(ks), shape, jnp.float32) * s
    gain = lambda shape: 1.0 + nrm(shape, 0.01)
    CL = min(BAND_PAST, PAST_LEN)
    u = jax.random.uniform(next(ks), (DEPTH, D_LRU), jnp.float32, minval=0.9, maxval=0.999)
    a_base = u ** (1.0 / LRU_C)
    return {
        "x_prompt": nrm((BATCH, SEQ, D_MODEL), 1.0),
        "x_sample": nrm((DEC_BATCH, DEC_SEQ, D_MODEL), 1.0),
        "state_conv": nrm((DEPTH, DEC_BATCH, CONV_W - 1, D_LRU), 1.0),
        "state_lru": nrm((DEPTH, DEC_BATCH, D_LRU), 0.5),
        "state_hgrn": nrm((DEPTH, DEC_BATCH, N_HGRN_HEADS, HGRN_HEAD_DIM, HGRN_HEAD_DIM), 0.5),
        "cache_k": nrm((DEPTH, DEC_BATCH, CL, N_ATT_HEADS, ATT_HEAD_DIM), 1.0),
        "cache_v": nrm((DEPTH, DEC_BATCH, CL, N_ATT_HEADS, ATT_HEAD_DIM), 1.0),
        "norm_mix": gain((DEPTH, D_MODEL)),
        "w_in": nrm((DEPTH, D_MODEL, D_IN), D_MODEL ** -0.5),
        "conv_w": nrm((DEPTH, CONV_W, D_LRU), CONV_W ** -0.5),
        "conv_b": nrm((DEPTH, D_LRU), 0.02),
        "lru_wr": nrm((DEPTH, N_LRU_BLOCKS, LRU_BLOCK, LRU_BLOCK), LRU_BLOCK ** -0.5),
        "lru_br": nrm((DEPTH, D_LRU), 0.02),
        "lru_wi": nrm((DEPTH, N_LRU_BLOCKS, LRU_BLOCK, LRU_BLOCK), LRU_BLOCK ** -0.5),
        "lru_bi": nrm((DEPTH, D_LRU), 0.02),
        "lru_lambda": jnp.log(a_base) - jnp.log1p(-a_base),
        "hgrn_gamma": nrm((DEPTH, D_HGRN), 0.1),
        "hgrn_norm": gain((DEPTH, D_HGRN)),
        "q_norm": gain((DEPTH, ATT_HEAD_DIM)),
        "k_norm": gain((DEPTH, ATT_HEAD_DIM)),
        "rel_bias_table": nrm((DEPTH, N_ATT_HEADS, N_REL), 0.1),
        "w_out": nrm((DEPTH, D_MODEL, D_MODEL), D_MODEL ** -0.5),
        "norm_ffn": gain((DEPTH, D_MODEL)),
        "ffn_w_gate": nrm((N_DENSE, D_MODEL, D_FF), D_MODEL ** -0.5),
        "ffn_w_up": nrm((N_DENSE, D_MODEL, D_FF), D_MODEL ** -0.5),
        "ffn_w_down": nrm((N_DENSE, D_FF, D_MODEL), D_FF ** -0.5),
        "moe_router": nrm((N_MOE, D_MODEL, N_EXPERTS), D_MODEL ** -0.5),
        "moe_w_gate": nrm((N_MOE, N_EXPERTS, D_MODEL, D_FF_EXPERT), D_MODEL ** -0.5),
        "moe_w_up": nrm((N_MOE, N_EXPERTS, D_MODEL, D_FF_EXPERT), D_MODEL ** -0.5),
        "moe_w_down": nrm((N_MOE, N_EXPERTS, D_FF_EXPERT, D_MODEL), D_FF_EXPERT ** -0.5),
    }


def reference(x_prompt, x_sample, state_conv, state_lru, state_hgrn, cache_k, cache_v,
              norm_mix, w_in, conv_w, conv_b, lru_wr, lru_br, lru_wi, lru_bi, lru_lambda,
              hgrn_gamma, hgrn_norm, q_norm, k_norm, rel_bias_table, w_out, norm_ffn,
              ffn_w_gate, ffn_w_up, ffn_w_down, moe_router, moe_w_gate, moe_w_up, moe_w_down):
    lb_all = hgrn_lower_bounds(hgrn_gamma)
    bp = x_prompt.shape[0]
    dt = x_prompt.dtype
    zero_buf = jnp.zeros((bp, CONV_W - 1, D_LRU), dt)
    zero_h = jnp.zeros((bp, D_LRU), dt)
    zero_s = jnp.zeros((bp, N_HGRN_HEADS, HGRN_HEAD_DIM, HGRN_HEAD_DIM), jnp.float32)
    yp, ys = x_prompt, x_sample
    st_p, st_s = [], []
    for l in range(DEPTH):
        w = (w_in[l], conv_w[l], conv_b[l], lru_wr[l], lru_br[l], lru_wi[l], lru_bi[l], lru_lambda[l],
             lb_all[l], hgrn_norm[l], q_norm[l], k_norm[l], rel_bias_table[l], w_out[l])
        o_p, sp = mixer(rms_norm(yp, norm_mix[l]), *w, zero_buf, zero_h, zero_s, None, None)
        o_s, ss = mixer(rms_norm(ys, norm_mix[l]), *w, state_conv[l], state_lru[l], state_hgrn[l],
                        cache_k[l], cache_v[l])
        yp = yp + o_p
        ys = ys + o_s
        yp = yp + channel_mixer(l, rms_norm(yp, norm_ffn[l]), ffn_w_gate, ffn_w_up, ffn_w_down,
                                moe_router, moe_w_gate, moe_w_up, moe_w_down)
        ys = ys + channel_mixer(l, rms_norm(ys, norm_ffn[l]), ffn_w_gate, ffn_w_up, ffn_w_down,
                                moe_router, moe_w_gate, moe_w_up, moe_w_down)
        st_p.append(sp)
        st_s.append(ss)
    conv_p = jnp.stack([s[0] for s in st_p])
    lru_p = jnp.stack([s[1] for s in st_p])
    hgrn_p = jnp.stack([s[2] for s in st_p])
    k_p = jnp.stack([s[3] for s in st_p])
    v_p = jnp.stack([s[4] for s in st_p])
    conv_s = jnp.stack([s[0] for s in st_s])
    lru_s = jnp.stack([s[1] for s in st_s])
    hgrn_s = jnp.stack([s[2] for s in st_s])
    k_s = jnp.stack([s[3] for s in st_s])
    v_s = jnp.stack([s[4] for s in st_s])
    return (yp, ys, conv_p, lru_p, hgrn_p, k_p, v_p, conv_s, lru_s, hgrn_s, k_s, v_s)
```

```python
import functools

import numpy as np
import jax
import jax.numpy as jnp
from jax import lax
from jax.experimental import pallas as pl
from jax.experimental.pallas import tpu as pltpu

F32 = jnp.float32
BF16 = jnp.bfloat16

D_MODEL = 1024
CHUNK = 64
EPS = 1e-6
MIN_FORGET = 1e-20
N_LRU_BLOCKS = 8
LRU_BLOCK = D_MODEL // N_LRU_BLOCKS
CONV_W = 4
LRU_C = 8.0
HGRN_HEAD_DIM = 128
N_HGRN_HEADS = D_MODEL // HGRN_HEAD_DIM
ATT_HEAD_DIM = 64
N_ATT_HEADS = D_MODEL // ATT_HEAD_DIM
N_HEAD_PAIRS = N_ATT_HEADS // 2
ATT_SCALE = ATT_HEAD_DIM ** -0.5
PAST_CHUNKS = 8
BAND_PAST = PAST_CHUNKS * CHUNK
MAX_REL = 128
MASK_VALUE = -1e30
D_FF = 2816
N_EXPERTS = 8
N_PROJ_BLOCKS = 12

LANES = 128
SUBLANES = 8
VMEM_LIMIT_BYTES = 56 * 1024 * 1024


def _cparams(*semantics):
    return pltpu.CompilerParams(dimension_semantics=semantics,
                                vmem_limit_bytes=VMEM_LIMIT_BYTES)


def _split3(x):
    hi = x.astype(BF16)
    r1 = x - hi.astype(F32)
    mid = r1.astype(BF16)
    lo = (r1 - mid.astype(F32)).astype(BF16)
    return hi, mid, lo


def _dot(a, b):
    return jnp.dot(a, b, preferred_element_type=F32)


def _dot_nt(a, b):
    return lax.dot_general(a, b, (((1,), (1,)), ((), ())), preferred_element_type=F32)


def _dot_tn(a, b):
    return lax.dot_general(a, b, (((0,), (0,)), ((), ())), preferred_element_type=F32)


def _sigmoid(x):
    return 1.0 / (1.0 + jnp.exp(-x))


def _norm_matmul_kernel(x_ref, g_ref, w_ref, o_ref, xn_ref):
    @pl.when(pl.program_id(1) == 0)
    def _():
        x = x_ref[...]
        ms = jnp.mean(x * x, axis=-1, keepdims=True)
        xn_ref[...] = (x * lax.rsqrt(ms + EPS) * g_ref[...]).astype(BF16)

    o_ref[...] = _dot(xn_ref[...], w_ref[...])


def _norm_matmul(x, g, w):
    n, d = x.shape
    dout = w.shape[1]
    tm = min(n, 1024)
    tn = 1024
    return pl.pallas_call(
        _norm_matmul_kernel,
        grid=(n // tm, dout // tn),
        in_specs=[
            pl.BlockSpec((tm, d), lambda i, j: (i, 0)),
            pl.BlockSpec((1, d), lambda i, j: (0, 0)),
            pl.BlockSpec((d, tn), lambda i, j: (0, j)),
        ],
        out_specs=pl.BlockSpec((tm, tn), lambda i, j: (i, j)),
        out_shape=jax.ShapeDtypeStruct((n, dout), F32),
        scratch_shapes=[pltpu.VMEM((tm, d), BF16)],
        compiler_params=_cparams("parallel", "arbitrary"),
        name="norm_in_proj",
    )(x, g.reshape(1, d), w)


_XP_PAD = SUBLANES


def _lru_kernel(ax_ref, ag_ref, conv0_ref, h0_ref, cw_ref, cb_ref, wri_ref, br_ref, bi_ref,
                lam_ref, y_ref, conv_out_ref, h_out_ref, xp_ref, a_ref, u_ref, h_ref, hc_ref):
    i = pl.program_id(1)
    tt = ax_ref.shape[0]
    keep = CONV_W - 1

    @pl.when(i == 0)
    def _():
        xp_ref[_XP_PAD - keep:_XP_PAD, :] = conv0_ref[0]
        hc_ref[...] = h0_ref[0]

    x = ax_ref[...]
    xp_ref[_XP_PAD:_XP_PAD + tt, :] = x
    conv = cb_ref[...] + cw_ref[CONV_W - 1:CONV_W, :] * x
    for j in range(CONV_W - 1):
        off = _XP_PAD - (CONV_W - 1 - j)
        conv = conv + cw_ref[j:j + 1, :] * xp_ref[off:off + tt, :]
    tail = xp_ref[_XP_PAD + tt - keep:_XP_PAD + tt, :]
    xp_ref[_XP_PAD - keep:_XP_PAD, :] = tail
    conv_out_ref[0] = tail

    cb16 = conv.astype(BF16)
    r_parts, i_parts = [], []
    for n in range(N_LRU_BLOCKS):
        g = _dot(cb16[:, n * LRU_BLOCK:(n + 1) * LRU_BLOCK], wri_ref[n])
        r_parts.append(g[:, :LRU_BLOCK])
        i_parts.append(g[:, LRU_BLOCK:])
    r = _sigmoid(jnp.concatenate(r_parts, axis=-1) + br_ref[...])
    ig = _sigmoid(jnp.concatenate(i_parts, axis=-1) + bi_ref[...])
    lam = lam_ref[...]
    softplus_neg = jnp.maximum(-lam, 0.0) + jnp.log1p(jnp.exp(-jnp.abs(lam)))
    log_a = -LRU_C * r * softplus_neg
    a = jnp.exp(log_a)
    th = jnp.tanh(log_a)
    u = jnp.sqrt(jnp.maximum(-2.0 * th / (1.0 - th), 0.0)) * ig * conv

    row = lax.broadcasted_iota(jnp.int32, a.shape, 0) % SUBLANES
    d = 1
    while d < SUBLANES:
        ok = row >= d
        a_sh = jnp.where(ok, pltpu.roll(a, d, 0), 1.0)
        u_sh = jnp.where(ok, pltpu.roll(u, d, 0), 0.0)
        u = a * u_sh + u
        a = a * a_sh
        d *= 2
    a_ref[...] = a
    u_ref[...] = u

    def group(gi, h):
        r0 = pl.multiple_of(gi * SUBLANES, SUBLANES)
        hh = a_ref[pl.ds(r0, SUBLANES), :] * h + u_ref[pl.ds(r0, SUBLANES), :]
        h_ref[pl.ds(r0, SUBLANES), :] = hh
        return hh[SUBLANES - 1:SUBLANES, :]

    h_last = lax.fori_loop(0, tt // SUBLANES, group, hc_ref[...], unroll=4)
    hc_ref[...] = h_last
    h_out_ref[0] = h_last
    y_ref[...] = h_ref[...] * jax.nn.gelu(ag_ref[...])


def _lru_branch(proj, conv0, h0, conv_w, conv_b, wri, b_r, b_i, lam, batch, seq):
    d = D_MODEL
    tt = min(seq, 256)
    nt = seq // tt
    row = lambda b, i: b * nt + i
    vec = lambda: pl.BlockSpec((1, d), lambda b, i: (0, 0))
    return pl.pallas_call(
        _lru_kernel,
        grid=(batch, nt),
        in_specs=[
            pl.BlockSpec((tt, d), lambda b, i: (row(b, i), 0)),
            pl.BlockSpec((tt, d), lambda b, i: (row(b, i), 1)),
            pl.BlockSpec((1, CONV_W - 1, d), lambda b, i: (b, 0, 0)),
            pl.BlockSpec((1, 1, d), lambda b, i: (b, 0, 0)),
            pl.BlockSpec((CONV_W, d), lambda b, i: (0, 0)),
            vec(),
            pl.BlockSpec((N_LRU_BLOCKS, LRU_BLOCK, 2 * LRU_BLOCK), lambda b, i: (0, 0, 0)),
            vec(), vec(), vec(),
        ],
        out_specs=[
            pl.BlockSpec((tt, d), lambda b, i: (row(b, i), 0)),
            pl.BlockSpec((1, CONV_W - 1, d), lambda b, i: (b, 0, 0)),
            pl.BlockSpec((1, 1, d), lambda b, i: (b, 0, 0)),
        ],
        out_shape=[
            jax.ShapeDtypeStruct((batch * seq, d), F32),
            jax.ShapeDtypeStruct((batch, CONV_W - 1, d), F32),
            jax.ShapeDtypeStruct((batch, 1, d), F32),
        ],
        scratch_shapes=[
            pltpu.VMEM((_XP_PAD + tt, d), F32),
            pltpu.VMEM((tt, d), F32),
            pltpu.VMEM((tt, d), F32),
            pltpu.VMEM((tt, d), F32),
            pltpu.VMEM((1, d), F32),
        ],
        compiler_params=_cparams("parallel", "arbitrary"),
        name="lru_branch",
    )(proj, proj, conv0, h0.reshape(batch, 1, d), conv_w, conv_b.reshape(1, d), wri,
      b_r.reshape(1, d), b_i.reshape(1, d), lam.reshape(1, d))


_HGRN_LEVELS = (32, 16, 8, 4, 2, 1)


def _hgrn_masks():
    t = np.arange(CHUNK)[:, None]
    s = np.arange(CHUNK)[None, :]
    masks = []
    for g in _HGRN_LEVELS:
        masks.append((t // (2 * g) == s // (2 * g)) & (t % (2 * g) >= g) & (s % (2 * g) < g))
    masks.append(t == s)
    return np.stack(masks).astype(np.float32)


def _hgrn_kernel(q_ref, f_ref, v_ref, g_ref, s0_ref, lb_ref, gn_ref, tri_ref, mask_ref,
                 y_ref, s_out_ref, st_ref, b_ref):
    c = pl.program_id(1)

    @pl.when(c == 0)
    def _():
        for h in range(N_HGRN_HEADS):
            st_ref[h] = s0_ref[0, h].T

    lb = lb_ref[...]
    z = f_ref[...]
    f = lb + (1.0 - lb) * _sigmoid(z)
    logf = jnp.log(jnp.maximum(f, MIN_FORGET))
    kb = (1.0 - lb) * _sigmoid(-z)
    qin = q_ref[...]
    qs = qin * _sigmoid(qin)
    v16 = v_ref[...].astype(BF16)

    tri = tri_ref[...]
    hi, mid, lo = _split3(logf)
    b = _dot(tri, hi) + _dot(tri, mid) + _dot(tri, lo)
    b_ref[...] = b
    b_last = b_ref[CHUNK - 1:CHUNK, :]

    q_inter = (qs * jnp.exp(b)).astype(BF16)
    k_state = (kb * jnp.exp(b_last - b)).astype(BF16)

    rowi = lax.broadcasted_iota(jnp.int32, b.shape, 0)
    q_lv, k_lv = [], []
    for g in _HGRN_LEVELS:
        if g >= 4:
            blocks = [jnp.broadcast_to(b_ref[m * 2 * g + g - 1:m * 2 * g + g, :], (2 * g, b.shape[1]))
                      for m in range(CHUNK // (2 * g))]
            ref = jnp.concatenate(blocks, axis=0) if len(blocks) > 1 else blocks[0]
        elif g == 2:
            p = rowi % 4
            ref = jnp.where(p == 0, pltpu.roll(b, CHUNK - 1, 0),
                            jnp.where(p == 1, b, jnp.where(p == 2, pltpu.roll(b, 1, 0),
                                                           pltpu.roll(b, 2, 0))))
        else:
            ref = jnp.where(rowi % 2 == 0, b, pltpu.roll(b, 1, 0))
        fac = jnp.exp(-jnp.abs(b - ref))
        q_lv.append((qs * fac).astype(BF16))
        k_lv.append((kb * fac).astype(BF16))
    q_lv.append(qs.astype(BF16))
    k_lv.append(kb.astype(BF16))

    gn = gn_ref[...]
    gate = g_ref[...]
    gate = gate * _sigmoid(gate)
    for h in range(N_HGRN_HEADS):
        sl = slice(h * HGRN_HEAD_DIM, (h + 1) * HGRN_HEAD_DIM)
        st = st_ref[h]
        att = mask_ref[0] * _dot_nt(q_lv[0][:, sl], k_lv[0][:, sl])
        for l in range(1, len(q_lv)):
            att = att + mask_ref[l] * _dot_nt(q_lv[l][:, sl], k_lv[l][:, sl])
        o = _dot_nt(q_inter[:, sl], st.astype(BF16)) + _dot(att.astype(BF16), v16[:, sl])
        ms = jnp.mean(o * o, axis=-1, keepdims=True)
        y_ref[:, sl] = o * lax.rsqrt(ms + EPS) * gn[:, sl] * gate[:, sl]
        st_ref[h] = st * jnp.exp(b_last[:, sl]) + _dot_tn(v16[:, sl], k_state[:, sl])

    @pl.when(c == pl.num_programs(1) - 1)
    def _():
        for h in range(N_HGRN_HEADS):
            s_out_ref[0, h] = st_ref[h].T


def _hgrn_branch(proj, s0, lb, gn, batch, seq):
    d = D_MODEL
    nc = seq // CHUNK
    row = lambda b, c: b * nc + c
    tri = jnp.asarray(np.tril(np.ones((CHUNK, CHUNK), np.float32)), BF16)
    masks = jnp.asarray(_hgrn_masks())
    col = lambda j: pl.BlockSpec((CHUNK, d), lambda b, c: (row(b, c), j))
    state = pl.BlockSpec((1, N_HGRN_HEADS, HGRN_HEAD_DIM, HGRN_HEAD_DIM), lambda b, c: (b, 0, 0, 0))
    vec = pl.BlockSpec((1, d), lambda b, c: (0, 0))
    return pl.pallas_call(
        _hgrn_kernel,
        grid=(batch, nc),
        in_specs=[col(2), col(3), col(4), col(5), state, vec, vec,
                  pl.BlockSpec((CHUNK, CHUNK), lambda b, c: (0, 0)),
                  pl.BlockSpec(masks.shape, lambda b, c: (0, 0, 0))],
        out_specs=[pl.BlockSpec((CHUNK, d), lambda b, c: (row(b, c), 0)), state],
        out_shape=[jax.ShapeDtypeStruct((batch * seq, d), F32),
                   jax.ShapeDtypeStruct(s0.shape, F32)],
        scratch_shapes=[pltpu.VMEM((N_HGRN_HEADS, HGRN_HEAD_DIM, HGRN_HEAD_DIM), F32),
                        pltpu.VMEM((CHUNK, d), F32)],
        compiler_params=_cparams("parallel", "arbitrary"),
        name="hgrn_branch",
    )(proj, proj, proj, proj, s0, lb.reshape(1, d), gn.reshape(1, d), tri, masks)


def _head_indicator():
    e = np.zeros((D_MODEL, LANES), np.float32)
    e[np.arange(D_MODEL), np.arange(D_MODEL) // ATT_HEAD_DIM] = 1.0
    return e


def _qk_norm_kernel(q_ref, k_ref, v_ref, qg_ref, kg_ref, e_ref, et_ref,
                    q16_ref, k16_ref, v16_ref, k32_ref):
    e = e_ref[...]
    et = et_ref[...]

    def head_norm(x, gain):
        hi, mid, _ = _split3(x * x)
        ms = (_dot(hi, e) + _dot(mid, e)) * (1.0 / ATT_HEAD_DIM)
        rhi, rmid, _ = _split3(lax.rsqrt(ms + EPS))
        return x * (_dot(rhi, et) + _dot(rmid, et)) * gain

    q16_ref[...] = (head_norm(q_ref[...], qg_ref[...]) * ATT_SCALE).astype(BF16)
    kn = head_norm(k_ref[...], kg_ref[...])
    k32_ref[...] = kn
    k16_ref[...] = kn.astype(BF16)
    v16_ref[...] = v_ref[...].astype(BF16)


def _qk_norm(proj, q_gain, k_gain):
    n = proj.shape[0]
    d = D_MODEL
    tr = min(n, 512)
    e = _head_indicator()
    col = lambda j: pl.BlockSpec((tr, d), lambda i: (i, j))
    vec = pl.BlockSpec((1, d), lambda i: (0, 0))
    out = pl.BlockSpec((tr, d), lambda i: (i, 0))
    tile = lambda g: jnp.tile(g, N_ATT_HEADS).reshape(1, d)
    return pl.pallas_call(
        _qk_norm_kernel,
        grid=(n // tr,),
        in_specs=[col(6), col(7), col(8), vec, vec,
                  pl.BlockSpec((d, LANES), lambda i: (0, 0)),
                  pl.BlockSpec((LANES, d), lambda i: (0, 0))],
        out_specs=[out, out, out, out],
        out_shape=[jax.ShapeDtypeStruct((n, d), BF16)] * 3 + [jax.ShapeDtypeStruct((n, d), F32)],
        compiler_params=_cparams("parallel"),
        name="qk_norm",
    )(proj, proj, proj, tile(q_gain), tile(k_gain), jnp.asarray(e, BF16), jnp.asarray(e.T, BF16))


def _attn_kernel(*refs, n_kv, dynamic_valid, tq):
    q_ref = refs[0]
    k_refs = refs[1:1 + n_kv]
    v_refs = refs[1 + n_kv:1 + 2 * n_kv]
    bias_ref = refs[1 + 2 * n_kv]
    o_ref = refs[2 + 2 * n_kv]

    q = q_ref[...]
    k = jnp.concatenate([r[...] for r in k_refs], axis=0) if n_kv > 1 else k_refs[0][...]
    v = jnp.concatenate([r[...] for r in v_refs], axis=0) if n_kv > 1 else v_refs[0][...]
    lane = lax.broadcasted_iota(jnp.int32, q.shape, 1)
    lo_half = lane < ATT_HEAD_DIM
    zero = jnp.zeros_like(q)
    outs = []
    for hh, qm in enumerate((jnp.where(lo_half, q, zero), jnp.where(lo_half, zero, q))):
        s = _dot_nt(qm, k) + bias_ref[hh]
        if dynamic_valid:
            first_valid = BAND_PAST - pl.program_id(2) * tq
            colj = lax.broadcasted_iota(jnp.int32, s.shape, 1)
            s = jnp.where(colj >= first_valid, s, MASK_VALUE)
        m = jnp.max(s, axis=-1, keepdims=True)
        p = jnp.exp(s - m)
        denom = jnp.sum(p, axis=-1, keepdims=True)
        outs.append(_dot(p.astype(BF16), v) / denom)
    o_ref[...] = jnp.where(lo_half, outs[0], outs[1])


def _band_bias(table, tq, n_keys, banded):
    t = np.arange(tq)[:, None]
    j = np.arange(n_keys)[None, :]
    dist = t + BAND_PAST - j
    bias = table[:, np.clip(dist, -MAX_REL, MAX_REL) + MAX_REL]
    if banded:
        c0 = (t // CHUNK) * CHUNK
        in_band = (j >= c0) & (j < c0 + BAND_PAST + CHUNK)
        bias = jnp.where(jnp.asarray(in_band)[None], bias, MASK_VALUE)
    return bias


def _attention_prompt(q16, k16, v16, table, batch, seq):
    tq = 256
    nt = seq // tq
    n_kv = BAND_PAST // tq + 1
    bias = _band_bias(table, tq, n_kv * tq, True)
    kv = lambda back: pl.BlockSpec(
        (tq, LANES), lambda p, b, i: (b * nt + jnp.maximum(i - back, 0), p))
    kv_specs = [kv(n_kv - 1 - j) for j in range(n_kv)]
    return pl.pallas_call(
        functools.partial(_attn_kernel, n_kv=n_kv, dynamic_valid=True, tq=tq),
        grid=(N_HEAD_PAIRS, batch, nt),
        in_specs=[pl.BlockSpec((tq, LANES), lambda p, b, i: (b * nt + i, p))] + kv_specs + kv_specs
                 + [pl.BlockSpec((2, tq, n_kv * tq), lambda p, b, i: (p, 0, 0))],
        out_specs=pl.BlockSpec((tq, LANES), lambda p, b, i: (b * nt + i, p)),
        out_shape=jax.ShapeDtypeStruct((batch * seq, D_MODEL), F32),
        compiler_params=_cparams("parallel", "parallel", "arbitrary"),
        name="band_attention_prompt",
    )(q16, *([k16] * n_kv), *([v16] * n_kv), bias)


def _attention_sample(q16, k16, v16, cache_k16, cache_v16, table, batch, seq):
    cl = cache_k16.shape[0] // batch
    bias = _band_bias(table, seq, cl + seq, False)
    own = pl.BlockSpec((seq, LANES), lambda p, b: (b, p))
    cache = pl.BlockSpec((cl, LANES), lambda p, b: (b, p))
    return pl.pallas_call(
        functools.partial(_attn_kernel, n_kv=2, dynamic_valid=False, tq=seq),
        grid=(N_HEAD_PAIRS, batch),
        in_specs=[own, cache, own, cache, own,
                  pl.BlockSpec((2, seq, cl + seq), lambda p, b: (p, 0, 0))],
        out_specs=own,
        out_shape=jax.ShapeDtypeStruct((batch * seq, D_MODEL), F32),
        compiler_params=_cparams("parallel", "parallel"),
        name="band_attention_sample",
    )(q16, cache_k16, k16, cache_v16, v16, bias)


def _merge_kernel(ga_ref, gb_ref, gc_ref, ya_ref, yb_ref, yc_ref, x_ref, w_ref, gn_ref, *rest,
                  with_router):
    if with_router:
        wr_ref, x1_ref, xn_ref, gates_ref = rest
    else:
        x1_ref, xn_ref = rest
    merged = (_sigmoid(ga_ref[...]) * ya_ref[...] + _sigmoid(gb_ref[...]) * yb_ref[...]
              + _sigmoid(gc_ref[...]) * yc_ref[...])
    x1 = x_ref[...] + _dot(merged.astype(BF16), w_ref[...])
    x1_ref[...] = x1
    ms = jnp.mean(x1 * x1, axis=-1, keepdims=True)
    xn = x1 * lax.rsqrt(ms + EPS) * gn_ref[...]
    xn_ref[...] = xn.astype(BF16)
    if with_router:
        xh, xl, _ = _split3(xn)
        wh = wr_ref[0]
        wl = wr_ref[1]
        logits = _dot(xh, wh) + (_dot(xh, wl) + _dot(xl, wh))
        lane = lax.broadcasted_iota(jnp.int32, logits.shape, 1).astype(F32)
        neg = jnp.float32(-jnp.inf)
        logits = jnp.where(lane < N_EXPERTS, logits, neg)
        m1 = jnp.max(logits, axis=-1, keepdims=True)
        i1 = jnp.min(jnp.where(logits == m1, lane, float(LANES)), axis=-1, keepdims=True)
        rest_l = jnp.where(lane == i1, neg, logits)
        m2 = jnp.max(rest_l, axis=-1, keepdims=True)
        i2 = jnp.min(jnp.where(rest_l == m2, lane, float(LANES)), axis=-1, keepdims=True)
        e2 = jnp.exp(m2 - m1)
        p1 = 1.0 / (1.0 + e2)
        p2 = e2 / (1.0 + e2)
        gates_ref[...] = jnp.where(lane == i1, p1, jnp.where(lane == i2, p2, 0.0))


def _merge_out(proj, ya, yb, yc, x, w_out16, ffn_gain, router=None):
    n, d = x.shape
    tm = min(n, 256)
    col = lambda j: pl.BlockSpec((tm, d), lambda i: (i, j))
    rowspec = pl.BlockSpec((tm, d), lambda i: (i, 0))
    in_specs = [col(9), col(10), col(11), rowspec, rowspec, rowspec, rowspec,
                pl.BlockSpec((d, d), lambda i: (0, 0)),
                pl.BlockSpec((1, d), lambda i: (0, 0))]
    args = [proj, proj, proj, ya, yb, yc, x, w_out16, ffn_gain.reshape(1, d)]
    out_specs = [rowspec, rowspec]
    out_shape = [jax.ShapeDtypeStruct((n, d), F32), jax.ShapeDtypeStruct((n, d), BF16)]
    if router is not None:
        wr = jnp.pad(router, ((0, 0), (0, LANES - N_EXPERTS)))
        wh = wr.astype(BF16)
        wl = (wr - wh.astype(F32)).astype(BF16)
        in_specs.append(pl.BlockSpec((2, d, LANES), lambda i: (0, 0, 0)))
        args.append(jnp.stack([wh, wl]))
        out_specs.append(pl.BlockSpec((tm, LANES), lambda i: (i, 0)))
        out_shape.append(jax.ShapeDtypeStruct((n, LANES), F32))
    return pl.pallas_call(
        functools.partial(_merge_kernel, with_router=router is not None),
        grid=(n // tm,),
        in_specs=in_specs,
        out_specs=out_specs,
        out_shape=out_shape,
        compiler_params=_cparams("parallel"),
        name="merge_out_proj",
    )(*args)


_FF_BLOCK = D_FF // 2


def _ffn_kernel(*refs, gated):
    if gated:
        xn_ref, x1_ref, gates_ref, wg_ref, wu_ref, wd_ref, o_ref, acc_ref = refs
    else:
        xn_ref, x1_ref, wg_ref, wu_ref, wd_ref, o_ref, acc_ref = refs
    e = pl.program_id(1)
    j = pl.program_id(2)

    @pl.when((e == 0) & (j == 0))
    def _():
        acc_ref[...] = x1_ref[...]

    xn = xn_ref[...]
    hg = _dot(xn, wg_ref[0])
    hu = _dot(xn, wu_ref[0])
    h = (hg * _sigmoid(hg) * hu).astype(BF16)
    y = _dot(h, wd_ref[0])
    if gated:
        gates = gates_ref[...]
        lane = lax.broadcasted_iota(jnp.int32, gates.shape, 1)
        y = y * jnp.sum(jnp.where(lane == e, gates, 0.0), axis=-1, keepdims=True)
    acc_ref[...] += y

    @pl.when((e == pl.num_programs(1) - 1) & (j == pl.num_programs(2) - 1))
    def _():
        o_ref[...] = acc_ref[...]


def _ffn(xn16, x1, wg16, wu16, wd16, gates=None):
    n, d = x1.shape
    ne = wg16.shape[0]
    tm = min(n, 512)
    nj = D_FF // _FF_BLOCK
    rowspec = lambda w: pl.BlockSpec((tm, w), lambda i, e, j: (i, 0))
    in_specs = [rowspec(d), rowspec(d)]
    args = [xn16, x1]
    if gates is not None:
        in_specs.append(rowspec(LANES))
        args.append(gates)
    in_specs += [pl.BlockSpec((1, d, _FF_BLOCK), lambda i, e, j: (e, 0, j)),
                 pl.BlockSpec((1, d, _FF_BLOCK), lambda i, e, j: (e, 0, j)),
                 pl.BlockSpec((1, _FF_BLOCK, d), lambda i, e, j: (e, j, 0))]
    args += [wg16, wu16, wd16]
    return pl.pallas_call(
        functools.partial(_ffn_kernel, gated=gates is not None),
        grid=(n // tm, ne, nj),
        in_specs=in_specs,
        out_specs=rowspec(d),
        out_shape=jax.ShapeDtypeStruct((n, d), F32),
        scratch_shapes=[pltpu.VMEM((tm, d), F32)],
        compiler_params=_cparams("parallel", "arbitrary", "arbitrary"),
        name="swiglu_ffn",
    )(*args)


def _mixer(x, layer_w, conv0, h0, s0, cache, batch, seq):
    proj = _norm_matmul(x, layer_w["norm_mix"], layer_w["w_in"])
    ya, conv_new, h_new = _lru_branch(proj, conv0, h0, layer_w["conv_w"], layer_w["conv_b"],
                                      layer_w["wri"], layer_w["lru_br"], layer_w["lru_bi"],
                                      layer_w["lru_lambda"], batch, seq)
    yb, s_new = _hgrn_branch(proj, s0, layer_w["lb"], layer_w["hgrn_norm"], batch, seq)
    q16, k16, v16, k32 = _qk_norm(proj, layer_w["q_norm"], layer_w["k_norm"])
    if cache is None:
        yc = _attention_prompt(q16, k16, v16, layer_w["rel_table"], batch, seq)
    else:
        yc = _attention_sample(q16, k16, v16, cache[0], cache[1], layer_w["rel_table"], batch, seq)
    keep = seq if cache is not None else min(BAND_PAST, seq)
    heads = (batch, seq, N_ATT_HEADS, ATT_HEAD_DIM)
    k_new = k32.reshape(heads)[:, seq - keep:]
    v_new = proj[:, 8 * D_MODEL:9 * D_MODEL].reshape(heads)[:, seq - keep:]
    states = (conv_new, h_new.reshape(batch, D_MODEL), s_new, k_new, v_new)
    return proj, ya, yb, yc, states


def kernel(x_prompt, x_sample, state_conv, state_lru, state_hgrn, cache_k, cache_v, norm_mix, w_in, conv_w, conv_b, lru_wr, lru_br, lru_wi, lru_bi, lru_lambda, hgrn_gamma, hgrn_norm, q_norm, k_norm, rel_bias_table, w_out, norm_ffn, ffn_w_gate, ffn_w_up, ffn_w_down, moe_router, moe_w_gate, moe_w_up, moe_w_down):
    depth = w_in.shape[0]
    bp, sp, d = x_prompt.shape
    bs, ss, _ = x_sample.shape

    p = jax.nn.softmax(hgrn_gamma.astype(F32), axis=0)
    lb_all = jnp.cumsum(p, axis=0) - p

    xp = x_prompt.reshape(bp * sp, d)
    xs = x_sample.reshape(bs * ss, d)
    zero_conv = jnp.zeros((bp, CONV_W - 1, d), F32)
    zero_h = jnp.zeros((bp, d), F32)
    zero_s = jnp.zeros((bp,) + state_hgrn.shape[2:], F32)

    st_p, st_s = [], []
    for l in range(depth):
        layer_w = dict(
            norm_mix=norm_mix[l], w_in=w_in[l].astype(BF16), conv_w=conv_w[l], conv_b=conv_b[l],
            wri=jnp.concatenate([lru_wr[l], lru_wi[l]], axis=-1).astype(BF16),
            lru_br=lru_br[l], lru_bi=lru_bi[l], lru_lambda=lru_lambda[l], lb=lb_all[l],
            hgrn_norm=hgrn_norm[l], q_norm=q_norm[l], k_norm=k_norm[l],
            rel_table=rel_bias_table[l])
        w_out16 = w_out[l].astype(BF16)
        cl = cache_k.shape[2]
        cache = (cache_k[l].reshape(bs * cl, d).astype(BF16),
                 cache_v[l].reshape(bs * cl, d).astype(BF16))
        dense = l % 2 == 0
        if dense:
            wg16 = ffn_w_gate[l // 2][None].astype(BF16)
            wu16 = ffn_w_up[l // 2][None].astype(BF16)
            wd16 = ffn_w_down[l // 2][None].astype(BF16)
            router = None
        else:
            wg16 = moe_w_gate[l // 2].astype(BF16)
            wu16 = moe_w_up[l // 2].astype(BF16)
            wd16 = moe_w_down[l // 2].astype(BF16)
            router = moe_router[l // 2]

        new_x = []
        for x, conv0, h0, s0, cch, batch, seq, acc in (
                (xp, zero_conv, zero_h, zero_s, None, bp, sp, st_p),
                (xs, state_conv[l], state_lru[l], state_hgrn[l], cache, bs, ss, st_s)):
            proj, ya, yb, yc, states = _mixer(x, layer_w, conv0, h0, s0, cch, batch, seq)
            outs = _merge_out(proj, ya, yb, yc, x, w_out16, norm_ffn[l], router)
            gates = outs[2] if router is not None else None
            new_x.append(_ffn(outs[1], outs[0], wg16, wu16, wd16, gates))
            acc.append(states)
        xp, xs = new_x

    stack = lambda sts, i: jnp.stack([s[i] for s in sts])
    return (xp.reshape(bp, sp, d), xs.reshape(bs, ss, d),
            stack(st_p, 0), stack(st_p, 1), stack(st_p, 2), stack(st_p, 3), stack(st_p, 4),
            stack(st_s, 0), stack(st_s, 1), stack(st_s, 2), stack(st_s, 3), stack(st_s, 4))
```

```python
import functools

import numpy as np
import jax
import jax.numpy as jnp
from jax import lax
from jax.experimental import pallas as pl
from jax.experimental.pallas import tpu as pltpu

F32 = jnp.float32
BF16 = jnp.bfloat16

D_MODEL = 1024
CHUNK = 64
EPS = 1e-6
MIN_FORGET = 1e-20
N_LRU_BLOCKS = 8
LRU_BLOCK = D_MODEL // N_LRU_BLOCKS
CONV_W = 4
LRU_C = 8.0
HGRN_HEAD_DIM = 128
N_HGRN_HEADS = D_MODEL // HGRN_HEAD_DIM
ATT_HEAD_DIM = 64
N_ATT_HEADS = D_MODEL // ATT_HEAD_DIM
N_HEAD_PAIRS = N_ATT_HEADS // 2
ATT_SCALE = ATT_HEAD_DIM ** -0.5
PAST_CHUNKS = 8
BAND_PAST = PAST_CHUNKS * CHUNK
MAX_REL = 128
MASK_VALUE = -1e30
D_FF = 2816
N_EXPERTS = 8
TOP_K = 2
N_PROJ_BLOCKS = 12

LANES = 128
SUBLANES = 8
TOKEN_TILE_ROWS = D_MODEL // LANES
VMEM_LIMIT_BYTES = 56 * 1024 * 1024


def _cparams(*semantics):
    return pltpu.CompilerParams(dimension_semantics=semantics,
                                vmem_limit_bytes=VMEM_LIMIT_BYTES)


def _split3(x):
    hi = x.astype(BF16)
    r1 = x - hi.astype(F32)
    mid = r1.astype(BF16)
    lo = (r1 - mid.astype(F32)).astype(BF16)
    return hi, mid, lo


def _dot(a, b):
    return jnp.dot(a, b, preferred_element_type=F32)


def _dot_nt(a, b):
    return lax.dot_general(a, b, (((1,), (1,)), ((), ())), preferred_element_type=F32)


def _dot_tn(a, b):
    return lax.dot_general(a, b, (((0,), (0,)), ((), ())), preferred_element_type=F32)


def _sigmoid(x):
    return 1.0 / (1.0 + jnp.exp(-x))


def _sigmoid_tanh(x):
    return 0.5 * jnp.tanh(0.5 * x) + 0.5


def _silu(x):
    half = 0.5 * x
    return half * (1.0 + jnp.tanh(half))


def _norm_matmul_kernel(x_ref, g_ref, w_ref, o_ref, xn_ref):
    @pl.when(pl.program_id(1) == 0)
    def _():
        x = x_ref[...]
        ms = jnp.mean(x * x, axis=-1, keepdims=True)
        xn_ref[...] = (x * lax.rsqrt(ms + EPS) * g_ref[...]).astype(BF16)

    o_ref[...] = _dot(xn_ref[...], w_ref[...])


def _norm_matmul(x, g, w):
    n, d = x.shape
    dout = w.shape[1]
    tm = min(n, 1024)
    tn = 1024
    return pl.pallas_call(
        _norm_matmul_kernel,
        grid=(n // tm, dout // tn),
        in_specs=[
            pl.BlockSpec((tm, d), lambda i, j: (i, 0)),
            pl.BlockSpec((1, d), lambda i, j: (0, 0)),
            pl.BlockSpec((d, tn), lambda i, j: (0, j)),
        ],
        out_specs=pl.BlockSpec((tm, tn), lambda i, j: (i, j)),
        out_shape=jax.ShapeDtypeStruct((n, dout), F32),
        scratch_shapes=[pltpu.VMEM((tm, d), BF16)],
        compiler_params=_cparams("parallel", "arbitrary"),
        name="norm_in_proj",
    )(x, g.reshape(1, d), w)


_XP_PAD = SUBLANES


def _lru_kernel(ax_ref, ag_ref, conv0_ref, h0_ref, cw_ref, cb_ref, wri_ref, br_ref, bi_ref,
                lam_ref, y_ref, conv_out_ref, h_out_ref, xp_ref, a_ref, u_ref, h_ref, hc_ref):
    i = pl.program_id(1)
    tt = ax_ref.shape[0]
    keep = CONV_W - 1

    @pl.when(i == 0)
    def _():
        xp_ref[_XP_PAD - keep:_XP_PAD, :] = conv0_ref[0]
        hc_ref[...] = h0_ref[0]

    x = ax_ref[...]
    xp_ref[_XP_PAD:_XP_PAD + tt, :] = x
    conv = cb_ref[...] + cw_ref[CONV_W - 1:CONV_W, :] * x
    for j in range(CONV_W - 1):
        off = _XP_PAD - (CONV_W - 1 - j)
        conv = conv + cw_ref[j:j + 1, :] * xp_ref[off:off + tt, :]
    tail = xp_ref[_XP_PAD + tt - keep:_XP_PAD + tt, :]
    xp_ref[_XP_PAD - keep:_XP_PAD, :] = tail
    conv_out_ref[0] = tail

    cb16 = conv.astype(BF16)
    r_parts, i_parts = [], []
    for n in range(N_LRU_BLOCKS):
        g = _dot(cb16[:, n * LRU_BLOCK:(n + 1) * LRU_BLOCK], wri_ref[n])
        r_parts.append(g[:, :LRU_BLOCK])
        i_parts.append(g[:, LRU_BLOCK:])
    r = _sigmoid(jnp.concatenate(r_parts, axis=-1) + br_ref[...])
    ig = _sigmoid(jnp.concatenate(i_parts, axis=-1) + bi_ref[...])
    lam = lam_ref[...]
    softplus_neg = jnp.maximum(-lam, 0.0) + jnp.log1p(jnp.exp(-jnp.abs(lam)))
    log_a = -LRU_C * r * softplus_neg
    a = jnp.exp(log_a)
    th = jnp.tanh(log_a)
    u = jnp.sqrt(jnp.maximum(-2.0 * th / (1.0 - th), 0.0)) * ig * conv

    row = lax.broadcasted_iota(jnp.int32, a.shape, 0) % SUBLANES
    d = 1
    while d < SUBLANES:
        ok = row >= d
        a_sh = jnp.where(ok, pltpu.roll(a, d, 0), 1.0)
        u_sh = jnp.where(ok, pltpu.roll(u, d, 0), 0.0)
        u = a * u_sh + u
        a = a * a_sh
        d *= 2
    a_ref[...] = a
    u_ref[...] = u

    def group(gi, h):
        r0 = pl.multiple_of(gi * SUBLANES, SUBLANES)
        hh = a_ref[pl.ds(r0, SUBLANES), :] * h + u_ref[pl.ds(r0, SUBLANES), :]
        h_ref[pl.ds(r0, SUBLANES), :] = hh
        return hh[SUBLANES - 1:SUBLANES, :]

    h_last = lax.fori_loop(0, tt // SUBLANES, group, hc_ref[...], unroll=4)
    hc_ref[...] = h_last
    h_out_ref[0] = h_last
    y_ref[...] = h_ref[...] * jax.nn.gelu(ag_ref[...])


def _lru_branch(proj, conv0, h0, conv_w, conv_b, wri, b_r, b_i, lam, batch, seq):
    d = D_MODEL
    tt = min(seq, 256)
    nt = seq // tt
    row = lambda b, i: b * nt + i
    vec = lambda: pl.BlockSpec((1, d), lambda b, i: (0, 0))
    return pl.pallas_call(
        _lru_kernel,
        grid=(batch, nt),
        in_specs=[
            pl.BlockSpec((tt, d), lambda b, i: (row(b, i), 0)),
            pl.BlockSpec((tt, d), lambda b, i: (row(b, i), 1)),
            pl.BlockSpec((1, CONV_W - 1, d), lambda b, i: (b, 0, 0)),
            pl.BlockSpec((1, 1, d), lambda b, i: (b, 0, 0)),
            pl.BlockSpec((CONV_W, d), lambda b, i: (0, 0)),
            vec(),
            pl.BlockSpec((N_LRU_BLOCKS, LRU_BLOCK, 2 * LRU_BLOCK), lambda b, i: (0, 0, 0)),
            vec(), vec(), vec(),
        ],
        out_specs=[
            pl.BlockSpec((tt, d), lambda b, i: (row(b, i), 0)),
            pl.BlockSpec((1, CONV_W - 1, d), lambda b, i: (b, 0, 0)),
            pl.BlockSpec((1, 1, d), lambda b, i: (b, 0, 0)),
        ],
        out_shape=[
            jax.ShapeDtypeStruct((batch * seq, d), F32),
            jax.ShapeDtypeStruct((batch, CONV_W - 1, d), F32),
            jax.ShapeDtypeStruct((batch, 1, d), F32),
        ],
        scratch_shapes=[
            pltpu.VMEM((_XP_PAD + tt, d), F32),
            pltpu.VMEM((tt, d), F32),
            pltpu.VMEM((tt, d), F32),
            pltpu.VMEM((tt, d), F32),
            pltpu.VMEM((1, d), F32),
        ],
        compiler_params=_cparams("parallel", "arbitrary"),
        name="lru_branch",
    )(proj, proj, conv0, h0.reshape(batch, 1, d), conv_w, conv_b.reshape(1, d), wri,
      b_r.reshape(1, d), b_i.reshape(1, d), lam.reshape(1, d))


_HGRN_LEVELS = (32, 16, 8, 4, 2, 1)
_HGRN_CHUNKS_PER_STEP = 4


def _hgrn_masks():
    t = np.arange(CHUNK)[:, None]
    s = np.arange(CHUNK)[None, :]
    masks = []
    for g in _HGRN_LEVELS:
        masks.append((t // (2 * g) == s // (2 * g)) & (t % (2 * g) >= g) & (s % (2 * g) < g))
    masks.append(t == s)
    return np.stack(masks).astype(np.float32)


def _hgrn_kernel(q_ref, f_ref, v_ref, g_ref, s0_ref, lb_ref, gn_ref, tri_ref, mask_ref,
                 y_ref, s_out_ref, st_ref, b_ref, *, chunks_per_step, n_steps):
    step = pl.program_id(1)

    def load_state():
        for h in range(N_HGRN_HEADS):
            st_ref[h] = s0_ref[0, h].T

    def store_state():
        for h in range(N_HGRN_HEADS):
            s_out_ref[0, h] = st_ref[h].T

    if n_steps == 1:
        load_state()
    else:
        pl.when(step == 0)(load_state)

    lb = lb_ref[...]
    gn = gn_ref[...]
    tri = tri_ref[...]
    rowi = lax.broadcasted_iota(jnp.int32, (CHUNK, D_MODEL), 0)

    for ci in range(chunks_per_step):
        rows = slice(ci * CHUNK, (ci + 1) * CHUNK)
        th = jnp.tanh(0.5 * f_ref[rows, :])
        f = lb + (1.0 - lb) * (0.5 + 0.5 * th)
        kb = (1.0 - lb) * (0.5 - 0.5 * th)
        qs = _silu(q_ref[rows, :])
        v16 = v_ref[rows, :].astype(BF16)

        hi, mid, lo = _split3(jnp.log2(jnp.maximum(f, MIN_FORGET)))
        b = _dot(tri, hi) + _dot(tri, mid) + _dot(tri, lo)
        bc_ref = b_ref.at[ci]
        bc_ref[...] = b
        b_last = bc_ref[CHUNK - 1:CHUNK, :]

        q_inter = (qs * jnp.exp2(b)).astype(BF16)
        k_state = (kb * jnp.exp2(b_last - b)).astype(BF16)

        q_lv, k_lv = [], []
        for g in _HGRN_LEVELS:
            if g >= 4:
                blocks = [jnp.broadcast_to(bc_ref[m * 2 * g + g - 1:m * 2 * g + g, :], (2 * g, D_MODEL))
                          for m in range(CHUNK // (2 * g))]
                ref = jnp.concatenate(blocks, axis=0) if len(blocks) > 1 else blocks[0]
            elif g == 2:
                p = rowi % 4
                ref = jnp.where(p == 0, pltpu.roll(b, CHUNK - 1, 0),
                                jnp.where(p == 1, b, jnp.where(p == 2, pltpu.roll(b, 1, 0),
                                                               pltpu.roll(b, 2, 0))))
            else:
                ref = jnp.where(rowi % 2 == 0, b, pltpu.roll(b, 1, 0))
            fac = jnp.exp2(-jnp.abs(b - ref))
            q_lv.append((qs * fac).astype(BF16))
            k_lv.append((kb * fac).astype(BF16))
        q_lv.append(qs.astype(BF16))
        k_lv.append(kb.astype(BF16))

        gate = _silu(g_ref[rows, :])
        for h in range(N_HGRN_HEADS):
            sl = slice(h * HGRN_HEAD_DIM, (h + 1) * HGRN_HEAD_DIM)
            st = st_ref[h]
            att = mask_ref[0] * _dot_nt(q_lv[0][:, sl], k_lv[0][:, sl])
            for l in range(1, len(q_lv)):
                att = att + mask_ref[l] * _dot_nt(q_lv[l][:, sl], k_lv[l][:, sl])
            o = _dot_nt(q_inter[:, sl], st.astype(BF16)) + _dot(att.astype(BF16), v16[:, sl])
            ms = jnp.mean(o * o, axis=-1, keepdims=True)
            y_ref[rows, sl] = o * lax.rsqrt(ms + EPS) * gn[:, sl] * gate[:, sl]
            st_ref[h] = st * jnp.exp2(b_last[:, sl]) + _dot_tn(v16[:, sl], k_state[:, sl])

    if n_steps == 1:
        store_state()
    else:
        pl.when(step == n_steps - 1)(store_state)


def _hgrn_branch(proj, s0, lb, gn, batch, seq):
    d = D_MODEL
    cps = min(seq // CHUNK, _HGRN_CHUNKS_PER_STEP)
    tt = cps * CHUNK
    nt = seq // tt
    row = lambda b, c: b * nt + c
    tri = jnp.asarray(np.tril(np.ones((CHUNK, CHUNK), np.float32)), BF16)
    masks = jnp.asarray(_hgrn_masks())
    col = lambda j: pl.BlockSpec((tt, d), lambda b, c: (row(b, c), j))
    state = lambda: pl.BlockSpec((1, N_HGRN_HEADS, HGRN_HEAD_DIM, HGRN_HEAD_DIM),
                                 lambda b, c: (b, 0, 0, 0))
    vec = lambda: pl.BlockSpec((1, d), lambda b, c: (0, 0))
    return pl.pallas_call(
        functools.partial(_hgrn_kernel, chunks_per_step=cps, n_steps=nt),
        grid=(batch, nt),
        in_specs=[col(2), col(3), col(4), col(5), state(), vec(), vec(),
                  pl.BlockSpec((CHUNK, CHUNK), lambda b, c: (0, 0)),
                  pl.BlockSpec(masks.shape, lambda b, c: (0, 0, 0))],
        out_specs=[pl.BlockSpec((tt, d), lambda b, c: (row(b, c), 0)), state()],
        out_shape=[jax.ShapeDtypeStruct((batch * seq, d), F32),
                   jax.ShapeDtypeStruct(s0.shape, F32)],
        scratch_shapes=[pltpu.VMEM((N_HGRN_HEADS, HGRN_HEAD_DIM, HGRN_HEAD_DIM), F32),
                        pltpu.VMEM((cps, CHUNK, d), F32)],
        compiler_params=_cparams("parallel", "arbitrary"),
        name="hgrn_branch",
    )(proj, proj, proj, proj, s0, lb.reshape(1, d), gn.reshape(1, d), tri, masks)


def _head_indicator():
    e = np.zeros((D_MODEL, LANES), np.float32)
    e[np.arange(D_MODEL), np.arange(D_MODEL) // ATT_HEAD_DIM] = 1.0
    return e


def _qk_norm_kernel(q_ref, k_ref, v_ref, qg_ref, kg_ref, e_ref, et_ref,
                    q16_ref, k16_ref, v16_ref, k32_ref):
    e = e_ref[...]
    et = et_ref[...]

    def head_norm(x, gain):
        hi, mid, _ = _split3(x * x)
        ms = (_dot(hi, e) + _dot(mid, e)) * (1.0 / ATT_HEAD_DIM)
        rhi, rmid, _ = _split3(lax.rsqrt(ms + EPS))
        return x * (_dot(rhi, et) + _dot(rmid, et)) * gain

    q16_ref[...] = (head_norm(q_ref[...], qg_ref[...]) * ATT_SCALE).astype(BF16)
    kn = head_norm(k_ref[...], kg_ref[...])
    k32_ref[...] = kn
    k16_ref[...] = kn.astype(BF16)
    v16_ref[...] = v_ref[...].astype(BF16)


def _qk_norm(proj, q_gain, k_gain):
    n = proj.shape[0]
    d = D_MODEL
    tr = min(n, 512)
    e = _head_indicator()
    col = lambda j: pl.BlockSpec((tr, d), lambda i: (i, j))
    vec = pl.BlockSpec((1, d), lambda i: (0, 0))
    out = pl.BlockSpec((tr, d), lambda i: (i, 0))
    tile = lambda g: jnp.tile(g, N_ATT_HEADS).reshape(1, d)
    return pl.pallas_call(
        _qk_norm_kernel,
        grid=(n // tr,),
        in_specs=[col(6), col(7), col(8), vec, vec,
                  pl.BlockSpec((d, LANES), lambda i: (0, 0)),
                  pl.BlockSpec((LANES, d), lambda i: (0, 0))],
        out_specs=[out, out, out, out],
        out_shape=[jax.ShapeDtypeStruct((n, d), BF16)] * 3 + [jax.ShapeDtypeStruct((n, d), F32)],
        compiler_params=_cparams("parallel"),
        name="qk_norm",
    )(proj, proj, proj, tile(q_gain), tile(k_gain), jnp.asarray(e, BF16), jnp.asarray(e.T, BF16))


_ATT_PAIRS_PER_STEP = 2
_ATT_LANES = _ATT_PAIRS_PER_STEP * LANES
_ATT_STEPS = N_HEAD_PAIRS // _ATT_PAIRS_PER_STEP


def _attn_kernel(*refs, n_kv, dynamic_valid, tq):
    q_ref = refs[0]
    k_refs = refs[1:1 + n_kv]
    v_refs = refs[1 + n_kv:1 + 2 * n_kv]
    bias_ref = refs[1 + 2 * n_kv]
    o_ref = refs[2 + 2 * n_kv]

    lane = lax.broadcasted_iota(jnp.int32, (tq, LANES), 1)
    lo_half = lane < ATT_HEAD_DIM
    for pi in range(q_ref.shape[1] // LANES):
        ls = slice(pi * LANES, (pi + 1) * LANES)
        q = q_ref[:, ls]
        k = jnp.concatenate([r[:, ls] for r in k_refs], axis=0) if n_kv > 1 else k_refs[0][:, ls]
        v = jnp.concatenate([r[:, ls] for r in v_refs], axis=0) if n_kv > 1 else v_refs[0][:, ls]
        zero = jnp.zeros_like(q)
        outs = []
        for hh, qm in enumerate((jnp.where(lo_half, q, zero), jnp.where(lo_half, zero, q))):
            s = _dot_nt(qm, k) + bias_ref[2 * pi + hh]
            if dynamic_valid:
                first_valid = BAND_PAST - pl.program_id(2) * tq
                colj = lax.broadcasted_iota(jnp.int32, s.shape, 1)
                s = jnp.where(colj >= first_valid, s, MASK_VALUE)
            m = jnp.max(s, axis=-1, keepdims=True)
            p = jnp.exp(s - m)
            denom = jnp.sum(p, axis=-1, keepdims=True)
            outs.append(_dot(p.astype(BF16), v) / denom)
        o_ref[:, ls] = jnp.where(lo_half, outs[0], outs[1])


def _band_bias(table, tq, n_keys, banded):
    t = np.arange(tq)[:, None]
    j = np.arange(n_keys)[None, :]
    width = tq + n_keys - 1
    dist = BAND_PAST + (tq - 1) - np.arange(width + 1)
    diag = table[:, np.clip(dist, -MAX_REL, MAX_REL) + MAX_REL]
    skew = jnp.tile(diag, (1, tq))[:, :tq * width].reshape(-1, tq, width)
    bias = skew[:, :, tq - 1:tq - 1 + n_keys]
    if banded:
        c0 = (t // CHUNK) * CHUNK
        in_band = (j >= c0) & (j < c0 + BAND_PAST + CHUNK)
        bias = jnp.where(jnp.asarray(in_band)[None], bias, MASK_VALUE)
    return bias


def _attention_prompt(q16, k16, v16, table, batch, seq):
    tq = 256
    nt = seq // tq
    n_kv = BAND_PAST // tq + 1
    bias = _band_bias(table, tq, n_kv * tq, True)
    kv = lambda back: pl.BlockSpec(
        (tq, _ATT_LANES), lambda p, b, i: (b * nt + jnp.maximum(i - back, 0), p))
    kv_specs = [kv(n_kv - 1 - j) for j in range(n_kv)]
    return pl.pallas_call(
        functools.partial(_attn_kernel, n_kv=n_kv, dynamic_valid=True, tq=tq),
        grid=(_ATT_STEPS, batch, nt),
        in_specs=[pl.BlockSpec((tq, _ATT_LANES), lambda p, b, i: (b * nt + i, p))] + kv_specs + kv_specs
                 + [pl.BlockSpec((2 * _ATT_PAIRS_PER_STEP, tq, n_kv * tq), lambda p, b, i: (p, 0, 0))],
        out_specs=pl.BlockSpec((tq, _ATT_LANES), lambda p, b, i: (b * nt + i, p)),
        out_shape=jax.ShapeDtypeStruct((batch * seq, D_MODEL), F32),
        compiler_params=_cparams("parallel", "parallel", "arbitrary"),
        name="band_attention_prompt",
    )(q16, *([k16] * n_kv), *([v16] * n_kv), bias)


def _attention_sample(q16, k16, v16, cache_k16, cache_v16, table, batch, seq):
    cl = cache_k16.shape[0] // batch
    bias = _band_bias(table, seq, cl + seq, False)
    own = pl.BlockSpec((seq, _ATT_LANES), lambda p, b: (b, p))
    cache = pl.BlockSpec((cl, _ATT_LANES), lambda p, b: (b, p))
    return pl.pallas_call(
        functools.partial(_attn_kernel, n_kv=2, dynamic_valid=False, tq=seq),
        grid=(_ATT_STEPS, batch),
        in_specs=[own, cache, own, cache, own,
                  pl.BlockSpec((2 * _ATT_PAIRS_PER_STEP, seq, cl + seq), lambda p, b: (p, 0, 0))],
        out_specs=own,
        out_shape=jax.ShapeDtypeStruct((batch * seq, D_MODEL), F32),
        compiler_params=_cparams("parallel", "parallel"),
        name="band_attention_sample",
    )(q16, cache_k16, k16, cache_v16, v16, bias)


def _store_token_tiles(ref, x):
    rows = x.shape[0]
    for s in range(TOKEN_TILE_ROWS):
        ref[pl.ds(s, rows, stride=TOKEN_TILE_ROWS), :] = x[:, s * LANES:(s + 1) * LANES]


def _load_token_tiles(ref, rows, s):
    return ref[pl.ds(s, rows, stride=TOKEN_TILE_ROWS), :]


def _merge_kernel(ga_ref, gb_ref, gc_ref, ya_ref, yb_ref, yc_ref, x_ref, w_ref, gn_ref, *rest,
                  with_router):
    if with_router:
        wr_ref, x1_ref, xn_ref, route_ref = rest
    else:
        x1_ref, xn_ref = rest
    merged = (_sigmoid_tanh(ga_ref[...]) * ya_ref[...] + _sigmoid_tanh(gb_ref[...]) * yb_ref[...]
              + _sigmoid_tanh(gc_ref[...]) * yc_ref[...])
    x1 = x_ref[...] + _dot(merged.astype(BF16), w_ref[...])
    x1_ref[...] = x1
    ms = jnp.mean(x1 * x1, axis=-1, keepdims=True)
    xn = x1 * lax.rsqrt(ms + EPS) * gn_ref[...]
    if not with_router:
        xn_ref[...] = xn.astype(BF16)
    else:
        _store_token_tiles(xn_ref, xn)
        xh, xl, _ = _split3(xn)
        wh = wr_ref[0]
        wl = wr_ref[1]
        logits = _dot(xh, wh) + (_dot(xh, wl) + _dot(xl, wh))
        lane = lax.broadcasted_iota(jnp.int32, logits.shape, 1).astype(F32)
        neg = jnp.float32(-jnp.inf)
        logits = jnp.where(lane < N_EXPERTS, logits, neg)
        m1 = jnp.max(logits, axis=-1, keepdims=True)
        i1 = jnp.min(jnp.where(logits == m1, lane, float(LANES)), axis=-1, keepdims=True)
        rest_l = jnp.where(lane == i1, neg, logits)
        m2 = jnp.max(rest_l, axis=-1, keepdims=True)
        i2 = jnp.min(jnp.where(rest_l == m2, lane, float(LANES)), axis=-1, keepdims=True)
        e2 = jnp.exp(m2 - m1)
        p1 = 1.0 / (1.0 + e2)
        p2 = e2 / (1.0 + e2)
        route_ref[...] = jnp.where(lane == 0.0, i1, jnp.where(lane == 1.0, i2,
                                   jnp.where(lane == 2.0, p1, jnp.where(lane == 3.0, p2, 0.0))))


def _merge_out(proj, ya, yb, yc, x, w_out16, ffn_gain, router=None):
    n, d = x.shape
    tm = min(n, 256)
    col = lambda j: pl.BlockSpec((tm, d), lambda i: (i, j))
    rowspec = pl.BlockSpec((tm, d), lambda i: (i, 0))
    in_specs = [col(9), col(10), col(11), rowspec, rowspec, rowspec, rowspec,
                pl.BlockSpec((d, d), lambda i: (0, 0)),
                pl.BlockSpec((1, d), lambda i: (0, 0))]
    args = [proj, proj, proj, ya, yb, yc, x, w_out16, ffn_gain.reshape(1, d)]
    out_specs = [rowspec, rowspec]
    out_shape = [jax.ShapeDtypeStruct((n, d), F32), jax.ShapeDtypeStruct((n, d), BF16)]
    if router is not None:
        out_specs[1] = pl.BlockSpec((tm * TOKEN_TILE_ROWS, LANES), lambda i: (i, 0))
        out_shape[1] = jax.ShapeDtypeStruct((n * TOKEN_TILE_ROWS, LANES), F32)
        wr = jnp.pad(router, ((0, 0), (0, LANES - N_EXPERTS)))
        wh = wr.astype(BF16)
        wl = (wr - wh.astype(F32)).astype(BF16)
        in_specs.append(pl.BlockSpec((2, d, LANES), lambda i: (0, 0, 0)))
        args.append(jnp.stack([wh, wl]))
        out_specs.append(pl.BlockSpec((tm, LANES), lambda i: (i, 0)))
        out_shape.append(jax.ShapeDtypeStruct((n, LANES), F32))
    return pl.pallas_call(
        functools.partial(_merge_kernel, with_router=router is not None),
        grid=(n // tm,),
        in_specs=in_specs,
        out_specs=out_specs,
        out_shape=out_shape,
        compiler_params=_cparams("parallel"),
        name="merge_out_proj",
    )(*args)


_FF_BLOCK = D_FF // 2
_N_FF_BLOCKS = D_FF // _FF_BLOCK


def _swiglu_block(xn16, wg, wu, wd):
    hg = _dot(xn16, wg)
    hu = _dot(xn16, wu)
    return _dot((hg * _sigmoid(hg) * hu).astype(BF16), wd)


def _ffn_kernel(xn_ref, x1_ref, wg_ref, wu_ref, wd_ref, o_ref, acc_ref):
    j = pl.program_id(1)
    y = _swiglu_block(xn_ref[...], wg_ref[...], wu_ref[...], wd_ref[...])

    @pl.when(j == 0)
    def _():
        acc_ref[...] = x1_ref[...] + y

    @pl.when(j > 0)
    def _():
        acc_ref[...] += y

    @pl.when(j == _N_FF_BLOCKS - 1)
    def _():
        o_ref[...] = acc_ref[...]


def _ffn(xn16, x1, wg16, wu16, wd16):
    n, d = x1.shape
    tm = min(n, 512)
    rowspec = lambda: pl.BlockSpec((tm, d), lambda i, j: (i, 0))
    return pl.pallas_call(
        _ffn_kernel,
        grid=(n // tm, _N_FF_BLOCKS),
        in_specs=[rowspec(), rowspec(),
                  pl.BlockSpec((d, _FF_BLOCK), lambda i, j: (0, j)),
                  pl.BlockSpec((d, _FF_BLOCK), lambda i, j: (0, j)),
                  pl.BlockSpec((_FF_BLOCK, d), lambda i, j: (j, 0))],
        out_specs=rowspec(),
        out_shape=jax.ShapeDtypeStruct((n, d), F32),
        scratch_shapes=[pltpu.VMEM((tm, d), F32)],
        compiler_params=_cparams("parallel", "arbitrary"),
        name="swiglu_ffn",
    )(xn16, x1, wg16, wu16, wd16)


_DISPATCH_CHUNK = 2048
_COMBINE_TOKENS = 256


def _routing_tables(route, n, tm):
    experts = route[:, :TOP_K].astype(jnp.int32).reshape(-1)
    onehot = (jnp.arange(N_EXPERTS, dtype=jnp.int32)[:, None] == experts[None]).astype(jnp.int32)
    running = jnp.cumsum(onehot, axis=1)
    rank = jnp.sum(running * onehot, axis=0) - 1
    count = running[:, -1]
    group = (count + tm - 1) // tm * tm
    group_end = jnp.cumsum(group)
    group_start = group_end - group
    dest = jnp.sum(onehot * group_start[:, None], axis=0) + rank
    n_tiles = TOP_K * n // tm + N_EXPERTS
    tile_start = jnp.arange(n_tiles, dtype=jnp.int32) * tm
    tile_expert = jnp.minimum(jnp.sum(tile_start[:, None] >= group_end[None], axis=1), N_EXPERTS - 1)
    n_used = (group_end[-1] // tm).reshape(1)
    return dest.astype(jnp.int32), tile_expert.astype(jnp.int32), n_used.astype(jnp.int32)


def _token_rows(index):
    return pl.ds(pl.multiple_of(index * TOKEN_TILE_ROWS, TOKEN_TILE_ROWS), TOKEN_TILE_ROWS)


def _dispatch_kernel(dest_ref, src_ref, init_ref, dst_ref, sem):
    del init_ref
    chunk = dest_ref.shape[2]
    base = pl.program_id(0) * chunk

    def issue(a, carry):
        token = lax.shift_right_logical(base + a, 1)
        pltpu.make_async_copy(src_ref.at[_token_rows(token)],
                              dst_ref.at[_token_rows(dest_ref[0, 0, a])], sem).start()
        return carry

    lax.fori_loop(0, chunk, issue, 0)
    rows = chunk * TOKEN_TILE_ROWS
    pltpu.make_async_copy(src_ref.at[pl.ds(0, rows)], dst_ref.at[pl.ds(0, rows)], sem).wait()


def _dispatch(xn_tiles, dest, n_slots):
    n_assign = dest.shape[0]
    chunk = min(n_assign, _DISPATCH_CHUNK)
    steps = n_assign // chunk
    shape = jax.ShapeDtypeStruct((n_slots * TOKEN_TILE_ROWS, LANES), F32)
    return pl.pallas_call(
        _dispatch_kernel,
        grid=(steps,),
        in_specs=[pl.BlockSpec((1, 1, chunk), lambda i: (i, 0, 0), memory_space=pltpu.SMEM),
                  pl.BlockSpec(memory_space=pl.ANY),
                  pl.BlockSpec(memory_space=pl.ANY)],
        out_specs=pl.BlockSpec(memory_space=pl.ANY),
        out_shape=shape,
        scratch_shapes=[pltpu.SemaphoreType.DMA(())],
        input_output_aliases={2: 0},
        compiler_params=_cparams("arbitrary"),
        name="moe_dispatch",
    )(dest.reshape(steps, 1, chunk), xn_tiles, jnp.zeros(shape.shape, F32))


def _expert_ffn_kernel(te_ref, nu_ref, x_ref, wg_ref, wu_ref, wd_ref, y_ref, xs_ref, acc_ref):
    del te_ref
    i = pl.program_id(0)
    j = pl.program_id(1)
    tm = xs_ref.shape[0]
    used = i < nu_ref[0]

    @pl.when(used & (j == 0))
    def _():
        for s in range(TOKEN_TILE_ROWS):
            xs_ref[:, s * LANES:(s + 1) * LANES] = _load_token_tiles(x_ref, tm, s).astype(BF16)

    @pl.when(used)
    def _():
        y = _swiglu_block(xs_ref[...], wg_ref[0], wu_ref[0], wd_ref[0])

        @pl.when(j == 0)
        def _():
            acc_ref[...] = y

        @pl.when(j > 0)
        def _():
            acc_ref[...] += y

    @pl.when(used & (j == _N_FF_BLOCKS - 1))
    def _():
        _store_token_tiles(y_ref, acc_ref[...])

    @pl.when(jnp.logical_not(used) & (j == _N_FF_BLOCKS - 1))
    def _():
        y_ref[...] = jnp.zeros(y_ref.shape, F32)


def _expert_ffn(x_sorted, tile_expert, n_used, wg16, wu16, wd16, tm):
    d = D_MODEL
    n_tiles = tile_expert.shape[0]
    rows = pl.BlockSpec((tm * TOKEN_TILE_ROWS, LANES), lambda i, j, te, nu: (i, 0))
    grid_spec = pltpu.PrefetchScalarGridSpec(
        num_scalar_prefetch=2,
        grid=(n_tiles, _N_FF_BLOCKS),
        in_specs=[rows,
                  pl.BlockSpec((1, d, _FF_BLOCK), lambda i, j, te, nu: (te[i], 0, j)),
                  pl.BlockSpec((1, d, _FF_BLOCK), lambda i, j, te, nu: (te[i], 0, j)),
                  pl.BlockSpec((1, _FF_BLOCK, d), lambda i, j, te, nu: (te[i], j, 0))],
        out_specs=rows,
        scratch_shapes=[pltpu.VMEM((tm, d), BF16), pltpu.VMEM((tm, d), F32)])
    return pl.pallas_call(
        _expert_ffn_kernel,
        grid_spec=grid_spec,
        out_shape=jax.ShapeDtypeStruct(x_sorted.shape, F32),
        compiler_params=_cparams("arbitrary", "arbitrary"),
        name="moe_expert_ffn",
    )(tile_expert, n_used, x_sorted, wg16, wu16, wd16)


def _combine_kernel(dest_ref, x1_ref, route_ref, y_ref, o_ref, buf_ref, sem):
    tc = x1_ref.shape[0]

    def issue(t, carry):
        for k in range(TOP_K):
            pltpu.make_async_copy(y_ref.at[_token_rows(dest_ref[0, 0, TOP_K * t + k])],
                                  buf_ref.at[k, _token_rows(t)], sem).start()
        return carry

    lax.fori_loop(0, tc, issue, 0)
    for k in range(TOP_K):
        pltpu.make_async_copy(y_ref.at[pl.ds(0, tc * TOKEN_TILE_ROWS)], buf_ref.at[k], sem).wait()
    route = route_ref[...]
    p1 = route[:, TOP_K:TOP_K + 1]
    p2 = route[:, TOP_K + 1:TOP_K + 2]
    for s in range(TOKEN_TILE_ROWS):
        sl = slice(s * LANES, (s + 1) * LANES)
        o_ref[:, sl] = (x1_ref[:, sl] + p1 * _load_token_tiles(buf_ref.at[0], tc, s)
                        + p2 * _load_token_tiles(buf_ref.at[1], tc, s))


def _combine(x1, route, y_sorted, dest):
    n, d = x1.shape
    tc = min(n, _COMBINE_TOKENS)
    steps = n // tc
    return pl.pallas_call(
        _combine_kernel,
        grid=(steps,),
        in_specs=[pl.BlockSpec((1, 1, TOP_K * tc), lambda i: (i, 0, 0), memory_space=pltpu.SMEM),
                  pl.BlockSpec((tc, d), lambda i: (i, 0)),
                  pl.BlockSpec((tc, LANES), lambda i: (i, 0)),
                  pl.BlockSpec(memory_space=pl.ANY)],
        out_specs=pl.BlockSpec((tc, d), lambda i: (i, 0)),
        out_shape=jax.ShapeDtypeStruct((n, d), F32),
        scratch_shapes=[pltpu.VMEM((TOP_K, tc * TOKEN_TILE_ROWS, LANES), F32),
                        pltpu.SemaphoreType.DMA(())],
        compiler_params=_cparams("arbitrary"),
        name="moe_combine",
    )(dest.reshape(steps, 1, TOP_K * tc), x1, route, y_sorted)


def _moe(xn_tiles, x1, route, wg16, wu16, wd16):
    n = x1.shape[0]
    tm = 512 if n >= 8192 else 256
    dest, tile_expert, n_used = _routing_tables(route, n, tm)
    x_sorted = _dispatch(xn_tiles, dest, tile_expert.shape[0] * tm)
    y_sorted = _expert_ffn(x_sorted, tile_expert, n_used, wg16, wu16, wd16, tm)
    return _combine(x1, route, y_sorted, dest)


def _mixer(x, layer_w, conv0, h0, s0, cache, batch, seq):
    proj = _norm_matmul(x, layer_w["norm_mix"], layer_w["w_in"])
    ya, conv_new, h_new = _lru_branch(proj, conv0, h0, layer_w["conv_w"], layer_w["conv_b"],
                                      layer_w["wri"], layer_w["lru_br"], layer_w["lru_bi"],
                                      layer_w["lru_lambda"], batch, seq)
    yb, s_new = _hgrn_branch(proj, s0, layer_w["lb"], layer_w["hgrn_norm"], batch, seq)
    q16, k16, v16, k32 = _qk_norm(proj, layer_w["q_norm"], layer_w["k_norm"])
    if cache is None:
        yc = _attention_prompt(q16, k16, v16, layer_w["rel_table"], batch, seq)
    else:
        yc = _attention_sample(q16, k16, v16, cache[0], cache[1], layer_w["rel_table"], batch, seq)
    keep = seq if cache is not None else min(BAND_PAST, seq)
    heads = (batch, seq, N_ATT_HEADS, ATT_HEAD_DIM)
    k_new = k32.reshape(heads)[:, seq - keep:]
    v_new = proj[:, 8 * D_MODEL:9 * D_MODEL].reshape(heads)[:, seq - keep:]
    states = (conv_new, h_new.reshape(batch, D_MODEL), s_new, k_new, v_new)
    return proj, ya, yb, yc, states


def kernel(x_prompt, x_sample, state_conv, state_lru, state_hgrn, cache_k, cache_v, norm_mix, w_in, conv_w, conv_b, lru_wr, lru_br, lru_wi, lru_bi, lru_lambda, hgrn_gamma, hgrn_norm, q_norm, k_norm, rel_bias_table, w_out, norm_ffn, ffn_w_gate, ffn_w_up, ffn_w_down, moe_router, moe_w_gate, moe_w_up, moe_w_down):
    depth = w_in.shape[0]
    bp, sp, d = x_prompt.shape
    bs, ss, _ = x_sample.shape

    p = jax.nn.softmax(hgrn_gamma.astype(F32), axis=0)
    lb_all = jnp.cumsum(p, axis=0) - p

    xp = x_prompt.reshape(bp * sp, d)
    xs = x_sample.reshape(bs * ss, d)
    zero_conv = jnp.zeros((bp, CONV_W - 1, d), F32)
    zero_h = jnp.zeros((bp, d), F32)
    zero_s = jnp.zeros((bp,) + state_hgrn.shape[2:], F32)

    st_p, st_s = [], []
    for l in range(depth):
        layer_w = dict(
            norm_mix=norm_mix[l], w_in=w_in[l].astype(BF16), conv_w=conv_w[l], conv_b=conv_b[l],
            wri=jnp.concatenate([lru_wr[l], lru_wi[l]], axis=-1).astype(BF16),
            lru_br=lru_br[l], lru_bi=lru_bi[l], lru_lambda=lru_lambda[l], lb=lb_all[l],
            hgrn_norm=hgrn_norm[l], q_norm=q_norm[l], k_norm=k_norm[l],
            rel_table=rel_bias_table[l])
        w_out16 = w_out[l].astype(BF16)
        cl = cache_k.shape[2]
        cache = (cache_k[l].reshape(bs * cl, d).astype(BF16),
                 cache_v[l].reshape(bs * cl, d).astype(BF16))
        dense = l % 2 == 0
        if dense:
            wg16 = ffn_w_gate[l // 2].astype(BF16)
            wu16 = ffn_w_up[l // 2].astype(BF16)
            wd16 = ffn_w_down[l // 2].astype(BF16)
            router = None
        else:
            wg16 = moe_w_gate[l // 2].astype(BF16)
            wu16 = moe_w_up[l // 2].astype(BF16)
            wd16 = moe_w_down[l // 2].astype(BF16)
            router = moe_router[l // 2]

        new_x = []
        for x, conv0, h0, s0, cch, batch, seq, acc in (
                (xp, zero_conv, zero_h, zero_s, None, bp, sp, st_p),
                (xs, state_conv[l], state_lru[l], state_hgrn[l], cache, bs, ss, st_s)):
            proj, ya, yb, yc, states = _mixer(x, layer_w, conv0, h0, s0, cch, batch, seq)
            outs = _merge_out(proj, ya, yb, yc, x, w_out16, norm_ffn[l], router)
            if router is None:
                new_x.append(_ffn(outs[1], outs[0], wg16, wu16, wd16))
            else:
                new_x.append(_moe(outs[1], outs[0], outs[2], wg16, wu16, wd16))
            acc.append(states)
        xp, xs = new_x

    stack = lambda sts, i: jnp.stack([s[i] for s in sts])
    return (xp.reshape(bp, sp, d), xs.reshape(bs, ss, d),
            stack(st_p, 0), stack(st_p, 1), stack(st_p, 2), stack(st_p, 3), stack(st_p, 4),
            stack(st_s, 0), stack(st_s, 1), stack(st_s, 2), stack(st_s, 3), stack(st_s, 4))
```

```python
import functools

import numpy as np
import jax
import jax.numpy as jnp
from jax import lax
from jax.experimental import pallas as pl
from jax.experimental.pallas import tpu as pltpu

F32 = jnp.float32
BF16 = jnp.bfloat16

D_MODEL = 1024
CHUNK = 64
EPS = 1e-6
MIN_FORGET = 1e-20
N_LRU_BLOCKS = 8
LRU_BLOCK = D_MODEL // N_LRU_BLOCKS
CONV_W = 4
LRU_C = 8.0
HGRN_HEAD_DIM = 128
N_HGRN_HEADS = D_MODEL // HGRN_HEAD_DIM
ATT_HEAD_DIM = 64
N_ATT_HEADS = D_MODEL // ATT_HEAD_DIM
N_HEAD_PAIRS = N_ATT_HEADS // 2
ATT_SCALE = ATT_HEAD_DIM ** -0.5
PAST_CHUNKS = 8
BAND_PAST = PAST_CHUNKS * CHUNK
MAX_REL = 128
MASK_VALUE = -1e30
D_FF = 2816
N_EXPERTS = 8
TOP_K = 2
N_PROJ_BLOCKS = 12

LANES = 128
SUBLANES = 8
TOKEN_TILE_ROWS = D_MODEL // LANES
VMEM_LIMIT_BYTES = 56 * 1024 * 1024


def _cparams(*semantics):
    return pltpu.CompilerParams(dimension_semantics=semantics,
                                vmem_limit_bytes=VMEM_LIMIT_BYTES)


def _split3(x):
    hi = x.astype(BF16)
    r1 = x - hi.astype(F32)
    mid = r1.astype(BF16)
    lo = (r1 - mid.astype(F32)).astype(BF16)
    return hi, mid, lo


def _dot(a, b):
    return jnp.dot(a, b, preferred_element_type=F32)


def _dot_nt(a, b):
    return lax.dot_general(a, b, (((1,), (1,)), ((), ())), preferred_element_type=F32)


def _dot_tn(a, b):
    return lax.dot_general(a, b, (((0,), (0,)), ((), ())), preferred_element_type=F32)


def _sigmoid(x):
    return 1.0 / (1.0 + jnp.exp(-x))


def _sigmoid_tanh(x):
    return 0.5 * jnp.tanh(0.5 * x) + 0.5


def _silu(x):
    half = 0.5 * x
    return half * (1.0 + jnp.tanh(half))


def _norm_matmul_kernel(x_ref, g_ref, w_ref, o_ref, xn_ref):
    @pl.when(pl.program_id(1) == 0)
    def _():
        x = x_ref[...]
        ms = jnp.mean(x * x, axis=-1, keepdims=True)
        xn_ref[...] = (x * lax.rsqrt(ms + EPS) * g_ref[...]).astype(BF16)

    o_ref[...] = _dot(xn_ref[...], w_ref[...])


def _norm_matmul(x, g, w):
    n, d = x.shape
    dout = w.shape[1]
    tm = min(n, 1024)
    tn = 1024
    return pl.pallas_call(
        _norm_matmul_kernel,
        grid=(n // tm, dout // tn),
        in_specs=[
            pl.BlockSpec((tm, d), lambda i, j: (i, 0)),
            pl.BlockSpec((1, d), lambda i, j: (0, 0)),
            pl.BlockSpec((d, tn), lambda i, j: (0, j)),
        ],
        out_specs=pl.BlockSpec((tm, tn), lambda i, j: (i, j)),
        out_shape=jax.ShapeDtypeStruct((n, dout), F32),
        scratch_shapes=[pltpu.VMEM((tm, d), BF16)],
        compiler_params=_cparams("parallel", "arbitrary"),
        name="norm_in_proj",
    )(x, g.reshape(1, d), w)


_XP_PAD = SUBLANES


def _lru_kernel(ax_ref, ag_ref, conv0_ref, h0_ref, cw_ref, cb_ref, wri_ref, br_ref, bi_ref,
                lam_ref, y_ref, conv_out_ref, h_out_ref, xp_ref, a_ref, u_ref, h_ref, hc_ref):
    i = pl.program_id(1)
    tt = ax_ref.shape[0]
    keep = CONV_W - 1

    @pl.when(i == 0)
    def _():
        xp_ref[_XP_PAD - keep:_XP_PAD, :] = conv0_ref[0]
        hc_ref[...] = h0_ref[0]

    x = ax_ref[...]
    xp_ref[_XP_PAD:_XP_PAD + tt, :] = x
    conv = cb_ref[...] + cw_ref[CONV_W - 1:CONV_W, :] * x
    for j in range(CONV_W - 1):
        off = _XP_PAD - (CONV_W - 1 - j)
        conv = conv + cw_ref[j:j + 1, :] * xp_ref[off:off + tt, :]
    tail = xp_ref[_XP_PAD + tt - keep:_XP_PAD + tt, :]
    xp_ref[_XP_PAD - keep:_XP_PAD, :] = tail
    conv_out_ref[0] = tail

    cb16 = conv.astype(BF16)
    r_parts, i_parts = [], []
    for n in range(N_LRU_BLOCKS):
        g = _dot(cb16[:, n * LRU_BLOCK:(n + 1) * LRU_BLOCK], wri_ref[n])
        r_parts.append(g[:, :LRU_BLOCK])
        i_parts.append(g[:, LRU_BLOCK:])
    r = _sigmoid(jnp.concatenate(r_parts, axis=-1) + br_ref[...])
    ig = _sigmoid(jnp.concatenate(i_parts, axis=-1) + bi_ref[...])
    lam = lam_ref[...]
    softplus_neg = jnp.maximum(-lam, 0.0) + jnp.log1p(jnp.exp(-jnp.abs(lam)))
    log_a = -LRU_C * r * softplus_neg
    a = jnp.exp(log_a)
    th = jnp.tanh(log_a)
    u = jnp.sqrt(jnp.maximum(-2.0 * th / (1.0 - th), 0.0)) * ig * conv

    row = lax.broadcasted_iota(jnp.int32, a.shape, 0) % SUBLANES
    d = 1
    while d < SUBLANES:
        ok = row >= d
        a_sh = jnp.where(ok, pltpu.roll(a, d, 0), 1.0)
        u_sh = jnp.where(ok, pltpu.roll(u, d, 0), 0.0)
        u = a * u_sh + u
        a = a * a_sh
        d *= 2
    a_ref[...] = a
    u_ref[...] = u

    def group(gi, h):
        r0 = pl.multiple_of(gi * SUBLANES, SUBLANES)
        hh = a_ref[pl.ds(r0, SUBLANES), :] * h + u_ref[pl.ds(r0, SUBLANES), :]
        h_ref[pl.ds(r0, SUBLANES), :] = hh
        return hh[SUBLANES - 1:SUBLANES, :]

    h_last = lax.fori_loop(0, tt // SUBLANES, group, hc_ref[...], unroll=4)
    hc_ref[...] = h_last
    h_out_ref[0] = h_last
    y_ref[...] = h_ref[...] * jax.nn.gelu(ag_ref[...])


def _lru_branch(proj, conv0, h0, conv_w, conv_b, wri, b_r, b_i, lam, batch, seq):
    d = D_MODEL
    tt = min(seq, 256)
    nt = seq // tt
    row = lambda b, i: b * nt + i
    vec = lambda: pl.BlockSpec((1, d), lambda b, i: (0, 0))
    return pl.pallas_call(
        _lru_kernel,
        grid=(batch, nt),
        in_specs=[
            pl.BlockSpec((tt, d), lambda b, i: (row(b, i), 0)),
            pl.BlockSpec((tt, d), lambda b, i: (row(b, i), 1)),
            pl.BlockSpec((1, CONV_W - 1, d), lambda b, i: (b, 0, 0)),
            pl.BlockSpec((1, 1, d), lambda b, i: (b, 0, 0)),
            pl.BlockSpec((CONV_W, d), lambda b, i: (0, 0)),
            vec(),
            pl.BlockSpec((N_LRU_BLOCKS, LRU_BLOCK, 2 * LRU_BLOCK), lambda b, i: (0, 0, 0)),
            vec(), vec(), vec(),
        ],
        out_specs=[
            pl.BlockSpec((tt, d), lambda b, i: (row(b, i), 0)),
            pl.BlockSpec((1, CONV_W - 1, d), lambda b, i: (b, 0, 0)),
            pl.BlockSpec((1, 1, d), lambda b, i: (b, 0, 0)),
        ],
        out_shape=[
            jax.ShapeDtypeStruct((batch * seq, d), F32),
            jax.ShapeDtypeStruct((batch, CONV_W - 1, d), F32),
            jax.ShapeDtypeStruct((batch, 1, d), F32),
        ],
        scratch_shapes=[
            pltpu.VMEM((_XP_PAD + tt, d), F32),
            pltpu.VMEM((tt, d), F32),
            pltpu.VMEM((tt, d), F32),
            pltpu.VMEM((tt, d), F32),
            pltpu.VMEM((1, d), F32),
        ],
        compiler_params=_cparams("parallel", "arbitrary"),
        name="lru_branch",
    )(proj, proj, conv0, h0.reshape(batch, 1, d), conv_w, conv_b.reshape(1, d), wri,
      b_r.reshape(1, d), b_i.reshape(1, d), lam.reshape(1, d))


_HGRN_BLOCK = 128
_HGRN_BLOCKS_PER_STEP = 4


def _hgrn_levels(block):
    return tuple(block >> (i + 1) for i in range(block.bit_length() - 1))


def _hgrn_masks(block):
    t = np.arange(block)[:, None]
    s = np.arange(block)[None, :]
    masks = []
    for g in _hgrn_levels(block):
        masks.append((t // (2 * g) == s // (2 * g)) & (t % (2 * g) >= g) & (s % (2 * g) < g))
    masks.append(t == s)
    return np.stack(masks).astype(np.float32)


def _hgrn_kernel(q_ref, f_ref, v_ref, g_ref, s0_ref, lb_ref, gn_ref, tri_ref, mask_ref,
                 y_ref, s_out_ref, st_ref, b_ref, *, block, blocks_per_step, n_steps):
    step = pl.program_id(1)

    def load_state():
        for h in range(N_HGRN_HEADS):
            st_ref[h] = s0_ref[0, h].T

    def store_state():
        for h in range(N_HGRN_HEADS):
            s_out_ref[0, h] = st_ref[h].T

    if n_steps == 1:
        load_state()
    else:
        pl.when(step == 0)(load_state)

    lb = lb_ref[...]
    gn = gn_ref[...]
    tri = tri_ref[...]
    rowi = lax.broadcasted_iota(jnp.int32, (block, D_MODEL), 0)

    for ci in range(blocks_per_step):
        rows = slice(ci * block, (ci + 1) * block)
        th = jnp.tanh(0.5 * f_ref[rows, :])
        f = lb + (1.0 - lb) * (0.5 + 0.5 * th)
        kb = (1.0 - lb) * (0.5 - 0.5 * th)
        qs = _silu(q_ref[rows, :])
        v16 = v_ref[rows, :].astype(BF16)

        hi, mid, lo = _split3(jnp.log2(jnp.maximum(f, MIN_FORGET)))
        b = _dot(tri, hi) + _dot(tri, mid) + _dot(tri, lo)
        bc_ref = b_ref.at[ci]
        bc_ref[...] = b
        b_last = bc_ref[block - 1:block, :]

        q_inter = (qs * jnp.exp2(b)).astype(BF16)
        k_state = (kb * jnp.exp2(b_last - b)).astype(BF16)

        q_lv, k_lv = [], []
        for g in _hgrn_levels(block):
            if g >= 4:
                blocks = [jnp.broadcast_to(bc_ref[m * 2 * g + g - 1:m * 2 * g + g, :], (2 * g, D_MODEL))
                          for m in range(block // (2 * g))]
                ref = jnp.concatenate(blocks, axis=0) if len(blocks) > 1 else blocks[0]
            elif g == 2:
                p = rowi % 4
                ref = jnp.where(p == 0, pltpu.roll(b, block - 1, 0),
                                jnp.where(p == 1, b, jnp.where(p == 2, pltpu.roll(b, 1, 0),
                                                               pltpu.roll(b, 2, 0))))
            else:
                ref = jnp.where(rowi % 2 == 0, b, pltpu.roll(b, 1, 0))
            fac = jnp.exp2(-jnp.abs(b - ref))
            q_lv.append((qs * fac).astype(BF16))
            k_lv.append((kb * fac).astype(BF16))
        q_lv.append(qs.astype(BF16))
        k_lv.append(kb.astype(BF16))

        gate = _silu(g_ref[rows, :])
        for h in range(N_HGRN_HEADS):
            sl = slice(h * HGRN_HEAD_DIM, (h + 1) * HGRN_HEAD_DIM)
            st = st_ref[h]
            att = mask_ref[0] * _dot_nt(q_lv[0][:, sl], k_lv[0][:, sl])
            for l in range(1, len(q_lv)):
                att = att + mask_ref[l] * _dot_nt(q_lv[l][:, sl], k_lv[l][:, sl])
            o = _dot_nt(q_inter[:, sl], st.astype(BF16)) + _dot(att.astype(BF16), v16[:, sl])
            ms = jnp.mean(o * o, axis=-1, keepdims=True)
            y_ref[rows, sl] = o * lax.rsqrt(ms + EPS) * gn[:, sl] * gate[:, sl]
            st_ref[h] = st * jnp.exp2(b_last[:, sl]) + _dot_tn(v16[:, sl], k_state[:, sl])

    if n_steps == 1:
        store_state()
    else:
        pl.when(step == n_steps - 1)(store_state)


def _hgrn_branch(proj, s0, lb, gn, batch, seq):
    d = D_MODEL
    block = min(seq, _HGRN_BLOCK)
    bps = min(seq // block, _HGRN_BLOCKS_PER_STEP)
    tt = bps * block
    nt = seq // tt
    row = lambda b, c: b * nt + c
    tri = jnp.asarray(np.tril(np.ones((block, block), np.float32)), BF16)
    masks = jnp.asarray(_hgrn_masks(block))
    col = lambda j: pl.BlockSpec((tt, d), lambda b, c: (row(b, c), j))
    state = lambda: pl.BlockSpec((1, N_HGRN_HEADS, HGRN_HEAD_DIM, HGRN_HEAD_DIM),
                                 lambda b, c: (b, 0, 0, 0))
    vec = lambda: pl.BlockSpec((1, d), lambda b, c: (0, 0))
    return pl.pallas_call(
        functools.partial(_hgrn_kernel, block=block, blocks_per_step=bps, n_steps=nt),
        grid=(batch, nt),
        in_specs=[col(2), col(3), col(4), col(5), state(), vec(), vec(),
                  pl.BlockSpec((block, block), lambda b, c: (0, 0)),
                  pl.BlockSpec(masks.shape, lambda b, c: (0, 0, 0))],
        out_specs=[pl.BlockSpec((tt, d), lambda b, c: (row(b, c), 0)), state()],
        out_shape=[jax.ShapeDtypeStruct((batch * seq, d), F32),
                   jax.ShapeDtypeStruct(s0.shape, F32)],
        scratch_shapes=[pltpu.VMEM((N_HGRN_HEADS, HGRN_HEAD_DIM, HGRN_HEAD_DIM), F32),
                        pltpu.VMEM((bps, block, d), F32)],
        compiler_params=_cparams("parallel", "arbitrary"),
        name="hgrn_branch",
    )(proj, proj, proj, proj, s0, lb.reshape(1, d), gn.reshape(1, d), tri, masks)


def _head_indicator():
    e = np.zeros((D_MODEL, LANES), np.float32)
    e[np.arange(D_MODEL), np.arange(D_MODEL) // ATT_HEAD_DIM] = 1.0
    return e


def _qk_norm_kernel(q_ref, k_ref, v_ref, qg_ref, kg_ref, e_ref, et_ref,
                    q16_ref, k16_ref, v16_ref, k32_ref):
    e = e_ref[...]
    et = et_ref[...]

    def head_norm(x, gain):
        hi, mid, _ = _split3(x * x)
        ms = (_dot(hi, e) + _dot(mid, e)) * (1.0 / ATT_HEAD_DIM)
        rhi, rmid, _ = _split3(lax.rsqrt(ms + EPS))
        return x * (_dot(rhi, et) + _dot(rmid, et)) * gain

    q16_ref[...] = (head_norm(q_ref[...], qg_ref[...]) * ATT_SCALE).astype(BF16)
    kn = head_norm(k_ref[...], kg_ref[...])
    k32_ref[...] = kn
    k16_ref[...] = kn.astype(BF16)
    v16_ref[...] = v_ref[...].astype(BF16)


def _qk_norm(proj, q_gain, k_gain):
    n = proj.shape[0]
    d = D_MODEL
    tr = min(n, 512)
    e = _head_indicator()
    col = lambda j: pl.BlockSpec((tr, d), lambda i: (i, j))
    vec = pl.BlockSpec((1, d), lambda i: (0, 0))
    out = pl.BlockSpec((tr, d), lambda i: (i, 0))
    tile = lambda g: jnp.tile(g, N_ATT_HEADS).reshape(1, d)
    return pl.pallas_call(
        _qk_norm_kernel,
        grid=(n // tr,),
        in_specs=[col(6), col(7), col(8), vec, vec,
                  pl.BlockSpec((d, LANES), lambda i: (0, 0)),
                  pl.BlockSpec((LANES, d), lambda i: (0, 0))],
        out_specs=[out, out, out, out],
        out_shape=[jax.ShapeDtypeStruct((n, d), BF16)] * 3 + [jax.ShapeDtypeStruct((n, d), F32)],
        compiler_params=_cparams("parallel"),
        name="qk_norm",
    )(proj, proj, proj, tile(q_gain), tile(k_gain), jnp.asarray(e, BF16), jnp.asarray(e.T, BF16))


_ATT_PAIRS_PER_STEP = 4
_ATT_LANES = _ATT_PAIRS_PER_STEP * LANES
_ATT_STEPS = N_HEAD_PAIRS // _ATT_PAIRS_PER_STEP


def _attn_kernel(*refs, n_kv, dynamic_valid, tq):
    q_ref = refs[0]
    k_refs = refs[1:1 + n_kv]
    v_refs = refs[1 + n_kv:1 + 2 * n_kv]
    bias_ref = refs[1 + 2 * n_kv]
    o_ref = refs[2 + 2 * n_kv]

    lane = lax.broadcasted_iota(jnp.int32, (tq, LANES), 1)
    lo_half = lane < ATT_HEAD_DIM
    for pi in range(q_ref.shape[1] // LANES):
        ls = slice(pi * LANES, (pi + 1) * LANES)
        q = q_ref[:, ls]
        k = jnp.concatenate([r[:, ls] for r in k_refs], axis=0) if n_kv > 1 else k_refs[0][:, ls]
        v = jnp.concatenate([r[:, ls] for r in v_refs], axis=0) if n_kv > 1 else v_refs[0][:, ls]
        zero = jnp.zeros_like(q)
        outs = []
        for hh, qm in enumerate((jnp.where(lo_half, q, zero), jnp.where(lo_half, zero, q))):
            s = _dot_nt(qm, k) + bias_ref[2 * pi + hh]
            if dynamic_valid:
                first_valid = BAND_PAST - pl.program_id(2) * tq
                colj = lax.broadcasted_iota(jnp.int32, s.shape, 1)
                s = jnp.where(colj >= first_valid, s, MASK_VALUE)
            m = jnp.max(s, axis=-1, keepdims=True)
            p = jnp.exp(s - m)
            denom = jnp.sum(p, axis=-1, keepdims=True)
            outs.append(_dot(p.astype(BF16), v) / denom)
        o_ref[:, ls] = jnp.where(lo_half, outs[0], outs[1])


def _band_bias(table, tq, n_keys, banded):
    t = np.arange(tq)[:, None]
    j = np.arange(n_keys)[None, :]
    width = tq + n_keys - 1
    dist = BAND_PAST + (tq - 1) - np.arange(width + 1)
    diag = table[:, np.clip(dist, -MAX_REL, MAX_REL) + MAX_REL]
    skew = jnp.tile(diag, (1, tq))[:, :tq * width].reshape(-1, tq, width)
    bias = skew[:, :, tq - 1:tq - 1 + n_keys]
    if banded:
        c0 = (t // CHUNK) * CHUNK
        in_band = (j >= c0) & (j < c0 + BAND_PAST + CHUNK)
        bias = jnp.where(jnp.asarray(in_band)[None], bias, MASK_VALUE)
    return bias


def _attention_prompt(q16, k16, v16, table, batch, seq):
    tq = 256
    nt = seq // tq
    n_kv = BAND_PAST // tq + 1
    bias = _band_bias(table, tq, n_kv * tq, True)
    kv = lambda back: pl.BlockSpec(
        (tq, _ATT_LANES), lambda p, b, i: (b * nt + jnp.maximum(i - back, 0), p))
    kv_specs = [kv(n_kv - 1 - j) for j in range(n_kv)]
    return pl.pallas_call(
        functools.partial(_attn_kernel, n_kv=n_kv, dynamic_valid=True, tq=tq),
        grid=(_ATT_STEPS, batch, nt),
        in_specs=[pl.BlockSpec((tq, _ATT_LANES), lambda p, b, i: (b * nt + i, p))] + kv_specs + kv_specs
                 + [pl.BlockSpec((2 * _ATT_PAIRS_PER_STEP, tq, n_kv * tq), lambda p, b, i: (p, 0, 0))],
        out_specs=pl.BlockSpec((tq, _ATT_LANES), lambda p, b, i: (b * nt + i, p)),
        out_shape=jax.ShapeDtypeStruct((batch * seq, D_MODEL), F32),
        compiler_params=_cparams("parallel", "parallel", "arbitrary"),
        name="band_attention_prompt",
    )(q16, *([k16] * n_kv), *([v16] * n_kv), bias)


def _attention_sample(q16, k16, v16, cache_k16, cache_v16, table, batch, seq):
    cl = cache_k16.shape[0] // batch
    bias = _band_bias(table, seq, cl + seq, False)
    own = pl.BlockSpec((seq, _ATT_LANES), lambda p, b: (b, p))
    cache = pl.BlockSpec((cl, _ATT_LANES), lambda p, b: (b, p))
    return pl.pallas_call(
        functools.partial(_attn_kernel, n_kv=2, dynamic_valid=False, tq=seq),
        grid=(_ATT_STEPS, batch),
        in_specs=[own, cache, own, cache, own,
                  pl.BlockSpec((2 * _ATT_PAIRS_PER_STEP, seq, cl + seq), lambda p, b: (p, 0, 0))],
        out_specs=own,
        out_shape=jax.ShapeDtypeStruct((batch * seq, D_MODEL), F32),
        compiler_params=_cparams("parallel", "parallel"),
        name="band_attention_sample",
    )(q16, cache_k16, k16, cache_v16, v16, bias)


def _store_token_tiles(ref, x):
    rows = x.shape[0]
    for s in range(TOKEN_TILE_ROWS):
        ref[pl.ds(s, rows, stride=TOKEN_TILE_ROWS), :] = x[:, s * LANES:(s + 1) * LANES]


def _load_token_tiles(ref, rows, s):
    return ref[pl.ds(s, rows, stride=TOKEN_TILE_ROWS), :]


def _merge_kernel(ga_ref, gb_ref, gc_ref, ya_ref, yb_ref, yc_ref, x_ref, w_ref, gn_ref, *rest,
                  with_router):
    if with_router:
        wr_ref, x1_ref, xn_ref, route_ref = rest
    else:
        x1_ref, xn_ref = rest
    merged = (_sigmoid_tanh(ga_ref[...]) * ya_ref[...] + _sigmoid_tanh(gb_ref[...]) * yb_ref[...]
              + _sigmoid_tanh(gc_ref[...]) * yc_ref[...])
    x1 = x_ref[...] + _dot(merged.astype(BF16), w_ref[...])
    x1_ref[...] = x1
    ms = jnp.mean(x1 * x1, axis=-1, keepdims=True)
    xn = x1 * lax.rsqrt(ms + EPS) * gn_ref[...]
    if not with_router:
        xn_ref[...] = xn.astype(BF16)
    else:
        _store_token_tiles(xn_ref, xn)
        xh, xl, _ = _split3(xn)
        wh = wr_ref[0]
        wl = wr_ref[1]
        logits = _dot(xh, wh) + (_dot(xh, wl) + _dot(xl, wh))
        lane = lax.broadcasted_iota(jnp.int32, logits.shape, 1).astype(F32)
        neg = jnp.float32(-jnp.inf)
        logits = jnp.where(lane < N_EXPERTS, logits, neg)
        m1 = jnp.max(logits, axis=-1, keepdims=True)
        i1 = jnp.min(jnp.where(logits == m1, lane, float(LANES)), axis=-1, keepdims=True)
        rest_l = jnp.where(lane == i1, neg, logits)
        m2 = jnp.max(rest_l, axis=-1, keepdims=True)
        i2 = jnp.min(jnp.where(rest_l == m2, lane, float(LANES)), axis=-1, keepdims=True)
        e2 = jnp.exp(m2 - m1)
        p1 = 1.0 / (1.0 + e2)
        p2 = e2 / (1.0 + e2)
        route_ref[...] = jnp.where(lane == 0.0, i1, jnp.where(lane == 1.0, i2,
                                   jnp.where(lane == 2.0, p1, jnp.where(lane == 3.0, p2, 0.0))))


def _merge_out(proj, ya, yb, yc, x, w_out16, ffn_gain, router=None):
    n, d = x.shape
    tm = min(n, 256)
    col = lambda j: pl.BlockSpec((tm, d), lambda i: (i, j))
    rowspec = pl.BlockSpec((tm, d), lambda i: (i, 0))
    in_specs = [col(9), col(10), col(11), rowspec, rowspec, rowspec, rowspec,
                pl.BlockSpec((d, d), lambda i: (0, 0)),
                pl.BlockSpec((1, d), lambda i: (0, 0))]
    args = [proj, proj, proj, ya, yb, yc, x, w_out16, ffn_gain.reshape(1, d)]
    out_specs = [rowspec, rowspec]
    out_shape = [jax.ShapeDtypeStruct((n, d), F32), jax.ShapeDtypeStruct((n, d), BF16)]
    if router is not None:
        out_specs[1] = pl.BlockSpec((tm * TOKEN_TILE_ROWS, LANES), lambda i: (i, 0))
        out_shape[1] = jax.ShapeDtypeStruct((n * TOKEN_TILE_ROWS, LANES), F32)
        wr = jnp.pad(router, ((0, 0), (0, LANES - N_EXPERTS)))
        wh = wr.astype(BF16)
        wl = (wr - wh.astype(F32)).astype(BF16)
        in_specs.append(pl.BlockSpec((2, d, LANES), lambda i: (0, 0, 0)))
        args.append(jnp.stack([wh, wl]))
        out_specs.append(pl.BlockSpec((tm, LANES), lambda i: (i, 0)))
        out_shape.append(jax.ShapeDtypeStruct((n, LANES), F32))
    return pl.pallas_call(
        functools.partial(_merge_kernel, with_router=router is not None),
        grid=(n // tm,),
        in_specs=in_specs,
        out_specs=out_specs,
        out_shape=out_shape,
        compiler_params=_cparams("parallel"),
        name="merge_out_proj",
    )(*args)


_FF_BLOCK = D_FF // 2
_N_FF_BLOCKS = D_FF // _FF_BLOCK


def _swiglu_block(xn16, wg, wu, wd):
    hg = _dot(xn16, wg)
    hu = _dot(xn16, wu)
    return _dot((hg * _sigmoid(hg) * hu).astype(BF16), wd)


def _ffn_kernel(xn_ref, x1_ref, wg_ref, wu_ref, wd_ref, o_ref, acc_ref):
    j = pl.program_id(1)
    y = _swiglu_block(xn_ref[...], wg_ref[...], wu_ref[...], wd_ref[...])

    @pl.when(j == 0)
    def _():
        acc_ref[...] = x1_ref[...] + y

    @pl.when(j > 0)
    def _():
        acc_ref[...] += y

    @pl.when(j == _N_FF_BLOCKS - 1)
    def _():
        o_ref[...] = acc_ref[...]


def _ffn(xn16, x1, wg16, wu16, wd16):
    n, d = x1.shape
    tm = min(n, 512)
    rowspec = lambda: pl.BlockSpec((tm, d), lambda i, j: (i, 0))
    return pl.pallas_call(
        _ffn_kernel,
        grid=(n // tm, _N_FF_BLOCKS),
        in_specs=[rowspec(), rowspec(),
                  pl.BlockSpec((d, _FF_BLOCK), lambda i, j: (0, j)),
                  pl.BlockSpec((d, _FF_BLOCK), lambda i, j: (0, j)),
                  pl.BlockSpec((_FF_BLOCK, d), lambda i, j: (j, 0))],
        out_specs=rowspec(),
        out_shape=jax.ShapeDtypeStruct((n, d), F32),
        scratch_shapes=[pltpu.VMEM((tm, d), F32)],
        compiler_params=_cparams("parallel", "arbitrary"),
        name="swiglu_ffn",
    )(xn16, x1, wg16, wu16, wd16)


_DISPATCH_CHUNK = 2048
_COMBINE_TOKENS = 256


def _routing_tables(route, n, tm):
    experts = route[:, :TOP_K].astype(jnp.int32).reshape(-1)
    onehot = (jnp.arange(N_EXPERTS, dtype=jnp.int32)[:, None] == experts[None]).astype(jnp.int32)
    running = jnp.cumsum(onehot, axis=1)
    rank = jnp.sum(running * onehot, axis=0) - 1
    count = running[:, -1]
    group = (count + tm - 1) // tm * tm
    group_end = jnp.cumsum(group)
    group_start = group_end - group
    dest = jnp.sum(onehot * group_start[:, None], axis=0) + rank
    n_tiles = TOP_K * n // tm + N_EXPERTS
    tile_start = jnp.arange(n_tiles, dtype=jnp.int32) * tm
    tile_expert = jnp.minimum(jnp.sum(tile_start[:, None] >= group_end[None], axis=1), N_EXPERTS - 1)
    n_used = (group_end[-1] // tm).reshape(1)
    return dest.astype(jnp.int32), tile_expert.astype(jnp.int32), n_used.astype(jnp.int32)


def _token_rows(index):
    return pl.ds(pl.multiple_of(index * TOKEN_TILE_ROWS, TOKEN_TILE_ROWS), TOKEN_TILE_ROWS)


def _dispatch_kernel(dest_ref, src_ref, init_ref, dst_ref, sem):
    del init_ref
    chunk = dest_ref.shape[2]

    def issue(token, carry):
        for k in range(TOP_K):
            pltpu.make_async_copy(src_ref.at[_token_rows(token)],
                                  dst_ref.at[_token_rows(dest_ref[0, 0, TOP_K * token + k])],
                                  sem).start()
        return carry

    lax.fori_loop(0, chunk // TOP_K, issue, 0)
    for k in range(TOP_K):
        pltpu.make_async_copy(src_ref, dst_ref.at[pl.ds(0, src_ref.shape[0])], sem).wait()


def _dispatch(xn_tiles, dest, n_slots):
    n_assign = dest.shape[0]
    chunk = min(n_assign, _DISPATCH_CHUNK)
    steps = n_assign // chunk
    shape = jax.ShapeDtypeStruct((n_slots * TOKEN_TILE_ROWS, LANES), F32)
    return pl.pallas_call(
        _dispatch_kernel,
        grid=(steps,),
        in_specs=[pl.BlockSpec((1, 1, chunk), lambda i: (i, 0, 0), memory_space=pltpu.SMEM),
                  pl.BlockSpec((chunk // TOP_K * TOKEN_TILE_ROWS, LANES), lambda i: (i, 0)),
                  pl.BlockSpec(memory_space=pl.ANY)],
        out_specs=pl.BlockSpec(memory_space=pl.ANY),
        out_shape=shape,
        scratch_shapes=[pltpu.SemaphoreType.DMA(())],
        input_output_aliases={2: 0},
        compiler_params=_cparams("arbitrary"),
        name="moe_dispatch",
    )(dest.reshape(steps, 1, chunk), xn_tiles, jnp.zeros(shape.shape, F32))


def _expert_ffn_kernel(te_ref, nu_ref, x_ref, wg_ref, wu_ref, wd_ref, y_ref, xs_ref, acc_ref):
    del te_ref
    i = pl.program_id(0)
    j = pl.program_id(1)
    tm = xs_ref.shape[0]
    used = i < nu_ref[0]

    @pl.when(used & (j == 0))
    def _():
        for s in range(TOKEN_TILE_ROWS):
            xs_ref[:, s * LANES:(s + 1) * LANES] = _load_token_tiles(x_ref, tm, s).astype(BF16)

    @pl.when(used)
    def _():
        y = _swiglu_block(xs_ref[...], wg_ref[0], wu_ref[0], wd_ref[0])

        @pl.when(j == 0)
        def _():
            acc_ref[...] = y

        @pl.when(j > 0)
        def _():
            acc_ref[...] += y

    @pl.when(used & (j == _N_FF_BLOCKS - 1))
    def _():
        _store_token_tiles(y_ref, acc_ref[...])

    @pl.when(jnp.logical_not(used) & (j == _N_FF_BLOCKS - 1))
    def _():
        y_ref[...] = jnp.zeros(y_ref.shape, F32)


def _expert_ffn(x_sorted, tile_expert, n_used, wg16, wu16, wd16, tm):
    d = D_MODEL
    n_tiles = tile_expert.shape[0]
    rows = pl.BlockSpec((tm * TOKEN_TILE_ROWS, LANES), lambda i, j, te, nu: (i, 0))
    grid_spec = pltpu.PrefetchScalarGridSpec(
        num_scalar_prefetch=2,
        grid=(n_tiles, _N_FF_BLOCKS),
        in_specs=[rows,
                  pl.BlockSpec((1, d, _FF_BLOCK), lambda i, j, te, nu: (te[i], 0, j)),
                  pl.BlockSpec((1, d, _FF_BLOCK), lambda i, j, te, nu: (te[i], 0, j)),
                  pl.BlockSpec((1, _FF_BLOCK, d), lambda i, j, te, nu: (te[i], j, 0))],
        out_specs=rows,
        scratch_shapes=[pltpu.VMEM((tm, d), BF16), pltpu.VMEM((tm, d), F32)])
    return pl.pallas_call(
        _expert_ffn_kernel,
        grid_spec=grid_spec,
        out_shape=jax.ShapeDtypeStruct(x_sorted.shape, F32),
        compiler_params=_cparams("arbitrary", "arbitrary"),
        name="moe_expert_ffn",
    )(tile_expert, n_used, x_sorted, wg16, wu16, wd16)


def _combine_kernel(dest_ref, x1_ref, route_ref, y_ref, o_ref, buf_ref, sem):
    tc = x1_ref.shape[0]

    def issue(t, carry):
        for k in range(TOP_K):
            pltpu.make_async_copy(y_ref.at[_token_rows(dest_ref[0, 0, TOP_K * t + k])],
                                  buf_ref.at[k, _token_rows(t)], sem).start()
        return carry

    lax.fori_loop(0, tc, issue, 0)
    for k in range(TOP_K):
        pltpu.make_async_copy(y_ref.at[pl.ds(0, tc * TOKEN_TILE_ROWS)], buf_ref.at[k], sem).wait()
    route = route_ref[...]
    p1 = route[:, TOP_K:TOP_K + 1]
    p2 = route[:, TOP_K + 1:TOP_K + 2]
    for s in range(TOKEN_TILE_ROWS):
        sl = slice(s * LANES, (s + 1) * LANES)
        o_ref[:, sl] = (x1_ref[:, sl] + p1 * _load_token_tiles(buf_ref.at[0], tc, s)
                        + p2 * _load_token_tiles(buf_ref.at[1], tc, s))


def _combine(x1, route, y_sorted, dest):
    n, d = x1.shape
    tc = min(n, _COMBINE_TOKENS)
    steps = n // tc
    return pl.pallas_call(
        _combine_kernel,
        grid=(steps,),
        in_specs=[pl.BlockSpec((1, 1, TOP_K * tc), lambda i: (i, 0, 0), memory_space=pltpu.SMEM),
                  pl.BlockSpec((tc, d), lambda i: (i, 0)),
                  pl.BlockSpec((tc, LANES), lambda i: (i, 0)),
                  pl.BlockSpec(memory_space=pl.ANY)],
        out_specs=pl.BlockSpec((tc, d), lambda i: (i, 0)),
        out_shape=jax.ShapeDtypeStruct((n, d), F32),
        scratch_shapes=[pltpu.VMEM((TOP_K, tc * TOKEN_TILE_ROWS, LANES), F32),
                        pltpu.SemaphoreType.DMA(())],
        compiler_params=_cparams("arbitrary"),
        name="moe_combine",
    )(dest.reshape(steps, 1, TOP_K * tc), x1, route, y_sorted)


def _moe(xn_tiles, x1, route, wg16, wu16, wd16):
    n = x1.shape[0]
    tm = 512 if n >= 8192 else 256
    dest, tile_expert, n_used = _routing_tables(route, n, tm)
    x_sorted = _dispatch(xn_tiles, dest, tile_expert.shape[0] * tm)
    y_sorted = _expert_ffn(x_sorted, tile_expert, n_used, wg16, wu16, wd16, tm)
    return _combine(x1, route, y_sorted, dest)


def _mixer(x, layer_w, conv0, h0, s0, cache, batch, seq):
    proj = _norm_matmul(x, layer_w["norm_mix"], layer_w["w_in"])
    ya, conv_new, h_new = _lru_branch(proj, conv0, h0, layer_w["conv_w"], layer_w["conv_b"],
                                      layer_w["wri"], layer_w["lru_br"], layer_w["lru_bi"],
                                      layer_w["lru_lambda"], batch, seq)
    yb, s_new = _hgrn_branch(proj, s0, layer_w["lb"], layer_w["hgrn_norm"], batch, seq)
    q16, k16, v16, k32 = _qk_norm(proj, layer_w["q_norm"], layer_w["k_norm"])
    if cache is None:
        yc = _attention_prompt(q16, k16, v16, layer_w["rel_table"], batch, seq)
    else:
        yc = _attention_sample(q16, k16, v16, cache[0], cache[1], layer_w["rel_table"], batch, seq)
    keep = seq if cache is not None else min(BAND_PAST, seq)
    heads = (batch, seq, N_ATT_HEADS, ATT_HEAD_DIM)
    k_new = k32.reshape(heads)[:, seq - keep:]
    v_new = proj[:, 8 * D_MODEL:9 * D_MODEL].reshape(heads)[:, seq - keep:]
    states = (conv_new, h_new.reshape(batch, D_MODEL), s_new, k_new, v_new)
    return proj, ya, yb, yc, states


def kernel(x_prompt, x_sample, state_conv, state_lru, state_hgrn, cache_k, cache_v, norm_mix, w_in, conv_w, conv_b, lru_wr, lru_br, lru_wi, lru_bi, lru_lambda, hgrn_gamma, hgrn_norm, q_norm, k_norm, rel_bias_table, w_out, norm_ffn, ffn_w_gate, ffn_w_up, ffn_w_down, moe_router, moe_w_gate, moe_w_up, moe_w_down):
    depth = w_in.shape[0]
    bp, sp, d = x_prompt.shape
    bs, ss, _ = x_sample.shape

    p = jax.nn.softmax(hgrn_gamma.astype(F32), axis=0)
    lb_all = jnp.cumsum(p, axis=0) - p

    xp = x_prompt.reshape(bp * sp, d)
    xs = x_sample.reshape(bs * ss, d)
    zero_conv = jnp.zeros((bp, CONV_W - 1, d), F32)
    zero_h = jnp.zeros((bp, d), F32)
    zero_s = jnp.zeros((bp,) + state_hgrn.shape[2:], F32)

    st_p, st_s = [], []
    for l in range(depth):
        layer_w = dict(
            norm_mix=norm_mix[l], w_in=w_in[l].astype(BF16), conv_w=conv_w[l], conv_b=conv_b[l],
            wri=jnp.concatenate([lru_wr[l], lru_wi[l]], axis=-1).astype(BF16),
            lru_br=lru_br[l], lru_bi=lru_bi[l], lru_lambda=lru_lambda[l], lb=lb_all[l],
            hgrn_norm=hgrn_norm[l], q_norm=q_norm[l], k_norm=k_norm[l],
            rel_table=rel_bias_table[l])
        w_out16 = w_out[l].astype(BF16)
        cl = cache_k.shape[2]
        cache = (cache_k[l].reshape(bs * cl, d).astype(BF16),
                 cache_v[l].reshape(bs * cl, d).astype(BF16))
        dense = l % 2 == 0
        if dense:
            wg16 = ffn_w_gate[l // 2].astype(BF16)
            wu16 = ffn_w_up[l // 2].astype(BF16)
            wd16 = ffn_w_down[l // 2].astype(BF16)
            router = None
        else:
            wg16 = moe_w_gate[l // 2].astype(BF16)
            wu16 = moe_w_up[l // 2].astype(BF16)
            wd16 = moe_w_down[l // 2].astype(BF16)
            router = moe_router[l // 2]

        new_x = []
        for x, conv0, h0, s0, cch, batch, seq, acc in (
                (xp, zero_conv, zero_h, zero_s, None, bp, sp, st_p),
                (xs, state_conv[l], state_lru[l], state_hgrn[l], cache, bs, ss, st_s)):
            proj, ya, yb, yc, states = _mixer(x, layer_w, conv0, h0, s0, cch, batch, seq)
            outs = _merge_out(proj, ya, yb, yc, x, w_out16, norm_ffn[l], router)
            if router is None:
                new_x.append(_ffn(outs[1], outs[0], wg16, wu16, wd16))
            else:
                new_x.append(_moe(outs[1], outs[0], outs[2], wg16, wu16, wd16))
            acc.append(states)
        xp, xs = new_x

    stack = lambda sts, i: jnp.stack([s[i] for s in sts])
    return (xp.reshape(bp, sp, d), xs.reshape(bs, ss, d),
            stack(st_p, 0), stack(st_p, 1), stack(st_p, 2), stack(st_p, 3), stack(st_p, 4),
            stack(st_s, 0), stack(st_s, 1), stack(st_s, 2), stack(st_s, 3), stack(st_s, 4))
```

```python
import functools

import numpy as np
import jax
import jax.numpy as jnp
from jax import lax
from jax.experimental import pallas as pl
from jax.experimental.pallas import tpu as pltpu

F32 = jnp.float32
BF16 = jnp.bfloat16

D_MODEL = 1024
CHUNK = 64
EPS = 1e-6
MIN_FORGET = 1e-20
N_LRU_BLOCKS = 8
LRU_BLOCK = D_MODEL // N_LRU_BLOCKS
CONV_W = 4
LRU_C = 8.0
HGRN_HEAD_DIM = 128
N_HGRN_HEADS = D_MODEL // HGRN_HEAD_DIM
ATT_HEAD_DIM = 64
N_ATT_HEADS = D_MODEL // ATT_HEAD_DIM
N_HEAD_PAIRS = N_ATT_HEADS // 2
ATT_SCALE = ATT_HEAD_DIM ** -0.5
PAST_CHUNKS = 8
BAND_PAST = PAST_CHUNKS * CHUNK
MAX_REL = 128
MASK_VALUE = -1e30
D_FF = 2816
N_EXPERTS = 8
TOP_K = 2
N_PROJ_BLOCKS = 12

LANES = 128
SUBLANES = 8
TOKEN_TILE_ROWS = D_MODEL // LANES
VMEM_LIMIT_BYTES = 56 * 1024 * 1024


def _cparams(*semantics):
    return pltpu.CompilerParams(dimension_semantics=semantics,
                                vmem_limit_bytes=VMEM_LIMIT_BYTES)


def _split3(x):
    hi = x.astype(BF16)
    r1 = x - hi.astype(F32)
    mid = r1.astype(BF16)
    lo = (r1 - mid.astype(F32)).astype(BF16)
    return hi, mid, lo


def _dot(a, b):
    return jnp.dot(a, b, preferred_element_type=F32)


def _dot_nt(a, b):
    return lax.dot_general(a, b, (((1,), (1,)), ((), ())), preferred_element_type=F32)


def _dot_tn(a, b):
    return lax.dot_general(a, b, (((0,), (0,)), ((), ())), preferred_element_type=F32)


def _sigmoid(x):
    return 1.0 / (1.0 + jnp.exp(-x))


def _sigmoid_tanh(x):
    return 0.5 * jnp.tanh(0.5 * x) + 0.5


def _silu(x):
    half = 0.5 * x
    return half * (1.0 + jnp.tanh(half))


_Q_COL, _K_COL, _V_COL = 6, 7, 8
N_WIDE_BLOCKS = N_PROJ_BLOCKS - 3


def _head_indicator():
    e = np.zeros((D_MODEL, LANES), np.float32)
    e[np.arange(D_MODEL), np.arange(D_MODEL) // ATT_HEAD_DIM] = 1.0
    return e


def _in_proj_kernel(x_ref, g_ref, w_ref, qg_ref, kg_ref, e_ref, et_ref,
                    o_ref, q16_ref, k16_ref, v16_ref, k32_ref, v32_ref, xn_ref,
                    *, tiles_per_tail, keep):
    i = pl.program_id(0)
    j = pl.program_id(1)
    tm = x_ref.shape[0]

    @pl.when(j == 0)
    def _():
        x = x_ref[...]
        ms = jnp.mean(x * x, axis=-1, keepdims=True)
        xn_ref[...] = (x * lax.rsqrt(ms + EPS) * g_ref[...]).astype(BF16)

    acc = lambda: _dot(xn_ref[...], w_ref[...])
    is_tail = (i % tiles_per_tail) == tiles_per_tail - 1

    def head_norm(x, gain):
        hi, mid, _ = _split3(x * x)
        ms = (_dot(hi, e_ref[...]) + _dot(mid, e_ref[...])) * (1.0 / ATT_HEAD_DIM)
        rhi, rmid, _ = _split3(lax.rsqrt(ms + EPS))
        return x * (_dot(rhi, et_ref[...]) + _dot(rmid, et_ref[...])) * gain

    @pl.when((j < _Q_COL) | (j > _V_COL))
    def _():
        o_ref[...] = acc()

    @pl.when(j == _Q_COL)
    def _():
        q16_ref[...] = (head_norm(acc(), qg_ref[...]) * ATT_SCALE).astype(BF16)

    @pl.when(j == _K_COL)
    def _():
        kn = head_norm(acc(), kg_ref[...])
        k16_ref[...] = kn.astype(BF16)

        @pl.when(is_tail)
        def _():
            k32_ref[...] = kn[tm - keep:, :]

    @pl.when(j == _V_COL)
    def _():
        v = acc()
        v16_ref[...] = v.astype(BF16)

        @pl.when(is_tail)
        def _():
            v32_ref[...] = v[tm - keep:, :]


def _in_proj(x, g, w, q_gain, k_gain, seq, keep):
    n, d = x.shape
    tm = min(n, 1024)
    tn = d
    if keep == seq:
        tiles_per_tail, keep_rows = 1, tm
    else:
        assert seq % tm == 0 and keep <= tm
        tiles_per_tail, keep_rows = seq // tm, keep
    n_tail = n // (tiles_per_tail * tm) * keep_rows
    e = _head_indicator()
    tile_gain = lambda gain: jnp.tile(gain, N_ATT_HEADS).reshape(1, d)
    wide_col = lambda j: jnp.where(j < _Q_COL, j, jnp.maximum(j - 3, _Q_COL - 1))
    row16 = pl.BlockSpec((tm, d), lambda i, j: (i, 0))
    tail = pl.BlockSpec((keep_rows, d), lambda i, j: (i // tiles_per_tail, 0))
    vec = lambda: pl.BlockSpec((1, d), lambda i, j: (0, 0))
    return pl.pallas_call(
        functools.partial(_in_proj_kernel, tiles_per_tail=tiles_per_tail, keep=keep_rows),
        grid=(n // tm, N_PROJ_BLOCKS),
        in_specs=[
            pl.BlockSpec((tm, d), lambda i, j: (i, 0)),
            vec(),
            pl.BlockSpec((d, tn), lambda i, j: (0, j)),
            vec(), vec(),
            pl.BlockSpec((d, LANES), lambda i, j: (0, 0)),
            pl.BlockSpec((LANES, d), lambda i, j: (0, 0)),
        ],
        out_specs=[pl.BlockSpec((tm, tn), lambda i, j: (i, wide_col(j))),
                   row16, row16, row16, tail, tail],
        out_shape=[jax.ShapeDtypeStruct((n, N_WIDE_BLOCKS * d), F32)]
                  + [jax.ShapeDtypeStruct((n, d), BF16)] * 3
                  + [jax.ShapeDtypeStruct((n_tail, d), F32)] * 2,
        scratch_shapes=[pltpu.VMEM((tm, d), BF16)],
        compiler_params=_cparams("arbitrary", "arbitrary"),
        name="norm_in_proj",
    )(x, g.reshape(1, d), w, tile_gain(q_gain), tile_gain(k_gain),
      jnp.asarray(e, BF16), jnp.asarray(e.T, BF16))


_XP_PAD = SUBLANES


def _lru_kernel(ax_ref, ag_ref, conv0_ref, h0_ref, cw_ref, cb_ref, wri_ref, br_ref, bi_ref,
                lam_ref, y_ref, conv_out_ref, h_out_ref, xp_ref, a_ref, u_ref, h_ref, hc_ref):
    i = pl.program_id(1)
    tt = ax_ref.shape[0]
    keep = CONV_W - 1

    @pl.when(i == 0)
    def _():
        xp_ref[_XP_PAD - keep:_XP_PAD, :] = conv0_ref[0]
        hc_ref[...] = h0_ref[0]

    x = ax_ref[...]
    xp_ref[_XP_PAD:_XP_PAD + tt, :] = x
    conv = cb_ref[...] + cw_ref[CONV_W - 1:CONV_W, :] * x
    for j in range(CONV_W - 1):
        off = _XP_PAD - (CONV_W - 1 - j)
        conv = conv + cw_ref[j:j + 1, :] * xp_ref[off:off + tt, :]
    tail = xp_ref[_XP_PAD + tt - keep:_XP_PAD + tt, :]
    xp_ref[_XP_PAD - keep:_XP_PAD, :] = tail
    conv_out_ref[0] = tail

    cb16 = conv.astype(BF16)
    r_parts, i_parts = [], []
    for n in range(N_LRU_BLOCKS):
        g = _dot(cb16[:, n * LRU_BLOCK:(n + 1) * LRU_BLOCK], wri_ref[n])
        r_parts.append(g[:, :LRU_BLOCK])
        i_parts.append(g[:, LRU_BLOCK:])
    r = _sigmoid_tanh(jnp.concatenate(r_parts, axis=-1) + br_ref[...])
    ig = _sigmoid_tanh(jnp.concatenate(i_parts, axis=-1) + bi_ref[...])
    lam = lam_ref[...]
    softplus_neg = jnp.maximum(-lam, 0.0) + jnp.log1p(jnp.exp(-jnp.abs(lam)))
    log_a = -LRU_C * r * softplus_neg
    a = jnp.exp(log_a)
    th = jnp.tanh(log_a)
    u = jnp.sqrt(jnp.maximum(-2.0 * th / (1.0 - th), 0.0)) * ig * conv

    row = lax.broadcasted_iota(jnp.int32, a.shape, 0) % SUBLANES
    d = 1
    while d < SUBLANES:
        ok = row >= d
        a_sh = jnp.where(ok, pltpu.roll(a, d, 0), 1.0)
        u_sh = jnp.where(ok, pltpu.roll(u, d, 0), 0.0)
        u = a * u_sh + u
        a = a * a_sh
        d *= 2
    a_ref[...] = a
    u_ref[...] = u

    def group(gi, h):
        r0 = pl.multiple_of(gi * SUBLANES, SUBLANES)
        hh = a_ref[pl.ds(r0, SUBLANES), :] * h + u_ref[pl.ds(r0, SUBLANES), :]
        h_ref[pl.ds(r0, SUBLANES), :] = hh
        return hh[SUBLANES - 1:SUBLANES, :]

    h_last = lax.fori_loop(0, tt // SUBLANES, group, hc_ref[...], unroll=4)
    hc_ref[...] = h_last
    h_out_ref[0] = h_last
    y_ref[...] = h_ref[...] * jax.nn.gelu(ag_ref[...])


def _lru_branch(proj, conv0, h0, conv_w, conv_b, wri, b_r, b_i, lam, batch, seq):
    d = D_MODEL
    tt = min(seq, 256)
    nt = seq // tt
    row = lambda b, i: b * nt + i
    vec = lambda: pl.BlockSpec((1, d), lambda b, i: (0, 0))
    return pl.pallas_call(
        _lru_kernel,
        grid=(batch, nt),
        in_specs=[
            pl.BlockSpec((tt, d), lambda b, i: (row(b, i), 0)),
            pl.BlockSpec((tt, d), lambda b, i: (row(b, i), 1)),
            pl.BlockSpec((1, CONV_W - 1, d), lambda b, i: (b, 0, 0)),
            pl.BlockSpec((1, 1, d), lambda b, i: (b, 0, 0)),
            pl.BlockSpec((CONV_W, d), lambda b, i: (0, 0)),
            vec(),
            pl.BlockSpec((N_LRU_BLOCKS, LRU_BLOCK, 2 * LRU_BLOCK), lambda b, i: (0, 0, 0)),
            vec(), vec(), vec(),
        ],
        out_specs=[
            pl.BlockSpec((tt, d), lambda b, i: (row(b, i), 0)),
            pl.BlockSpec((1, CONV_W - 1, d), lambda b, i: (b, 0, 0)),
            pl.BlockSpec((1, 1, d), lambda b, i: (b, 0, 0)),
        ],
        out_shape=[
            jax.ShapeDtypeStruct((batch * seq, d), F32),
            jax.ShapeDtypeStruct((batch, CONV_W - 1, d), F32),
            jax.ShapeDtypeStruct((batch, 1, d), F32),
        ],
        scratch_shapes=[
            pltpu.VMEM((_XP_PAD + tt, d), F32),
            pltpu.VMEM((tt, d), F32),
            pltpu.VMEM((tt, d), F32),
            pltpu.VMEM((tt, d), F32),
            pltpu.VMEM((1, d), F32),
        ],
        compiler_params=_cparams("parallel", "arbitrary"),
        name="lru_branch",
    )(proj, proj, conv0, h0.reshape(batch, 1, d), conv_w, conv_b.reshape(1, d), wri,
      b_r.reshape(1, d), b_i.reshape(1, d), lam.reshape(1, d))


_HGRN_BLOCK = 128
_HGRN_BLOCKS_PER_STEP = 4


def _hgrn_levels(block):
    return tuple(block >> (i + 1) for i in range(block.bit_length() - 1))


def _hgrn_masks(block):
    t = np.arange(block)[:, None]
    s = np.arange(block)[None, :]
    masks = []
    for g in _hgrn_levels(block):
        masks.append((t // (2 * g) == s // (2 * g)) & (t % (2 * g) >= g) & (s % (2 * g) < g))
    masks.append(t == s)
    return np.stack(masks).astype(np.float32)


def _hgrn_kernel(q_ref, f_ref, v_ref, g_ref, s0_ref, lb_ref, gn_ref, tri_ref, mask_ref,
                 y_ref, s_out_ref, st_ref, b_ref, *, block, blocks_per_step, n_steps):
    step = pl.program_id(1)

    def load_state():
        for h in range(N_HGRN_HEADS):
            st_ref[h] = s0_ref[0, h].T

    def store_state():
        for h in range(N_HGRN_HEADS):
            s_out_ref[0, h] = st_ref[h].T

    if n_steps == 1:
        load_state()
    else:
        pl.when(step == 0)(load_state)

    lb = lb_ref[...]
    gn = gn_ref[...]
    tri = tri_ref[...]
    rowi = lax.broadcasted_iota(jnp.int32, (block, D_MODEL), 0)

    for ci in range(blocks_per_step):
        rows = slice(ci * block, (ci + 1) * block)
        th = jnp.tanh(0.5 * f_ref[rows, :])
        f = lb + (1.0 - lb) * (0.5 + 0.5 * th)
        kb = (1.0 - lb) * (0.5 - 0.5 * th)
        qs = _silu(q_ref[rows, :])
        v16 = v_ref[rows, :].astype(BF16)

        hi, mid, lo = _split3(jnp.log2(jnp.maximum(f, MIN_FORGET)))
        b = _dot(tri, hi) + _dot(tri, mid) + _dot(tri, lo)
        bc_ref = b_ref.at[ci]
        bc_ref[...] = b
        b_last = bc_ref[block - 1:block, :]

        q_inter = (qs * jnp.exp2(b)).astype(BF16)
        k_state = (kb * jnp.exp2(b_last - b)).astype(BF16)

        q_lv, k_lv = [], []
        for g in _hgrn_levels(block):
            if g >= 4:
                blocks = [jnp.broadcast_to(bc_ref[m * 2 * g + g - 1:m * 2 * g + g, :], (2 * g, D_MODEL))
                          for m in range(block // (2 * g))]
                ref = jnp.concatenate(blocks, axis=0) if len(blocks) > 1 else blocks[0]
            elif g == 2:
                p = rowi % 4
                ref = jnp.where(p == 0, pltpu.roll(b, block - 1, 0),
                                jnp.where(p == 1, b, jnp.where(p == 2, pltpu.roll(b, 1, 0),
                                                               pltpu.roll(b, 2, 0))))
            else:
                ref = jnp.where(rowi % 2 == 0, b, pltpu.roll(b, 1, 0))
            fac = jnp.exp2(-jnp.abs(b - ref))
            q_lv.append((qs * fac).astype(BF16))
            k_lv.append((kb * fac).astype(BF16))
        q_lv.append(qs.astype(BF16))
        k_lv.append(kb.astype(BF16))

        gate = _silu(g_ref[rows, :])
        for h in range(N_HGRN_HEADS):
            sl = slice(h * HGRN_HEAD_DIM, (h + 1) * HGRN_HEAD_DIM)
            st = st_ref[h]
            att = mask_ref[0] * _dot_nt(q_lv[0][:, sl], k_lv[0][:, sl])
            for l in range(1, len(q_lv)):
                att = att + mask_ref[l] * _dot_nt(q_lv[l][:, sl], k_lv[l][:, sl])
            o = _dot_nt(q_inter[:, sl], st.astype(BF16)) + _dot(att.astype(BF16), v16[:, sl])
            ms = jnp.mean(o * o, axis=-1, keepdims=True)
            y_ref[rows, sl] = o * lax.rsqrt(ms + EPS) * gn[:, sl] * gate[:, sl]
            st_ref[h] = st * jnp.exp2(b_last[:, sl]) + _dot_tn(v16[:, sl], k_state[:, sl])

    if n_steps == 1:
        store_state()
    else:
        pl.when(step == n_steps - 1)(store_state)


def _hgrn_branch(proj, s0, lb, gn, batch, seq):
    d = D_MODEL
    block = min(seq, _HGRN_BLOCK)
    bps = min(seq // block, _HGRN_BLOCKS_PER_STEP)
    tt = bps * block
    nt = seq // tt
    row = lambda b, c: b * nt + c
    tri = jnp.asarray(np.tril(np.ones((block, block), np.float32)), BF16)
    masks = jnp.asarray(_hgrn_masks(block))
    col = lambda j: pl.BlockSpec((tt, d), lambda b, c: (row(b, c), j))
    state = lambda: pl.BlockSpec((1, N_HGRN_HEADS, HGRN_HEAD_DIM, HGRN_HEAD_DIM),
                                 lambda b, c: (b, 0, 0, 0))
    vec = lambda: pl.BlockSpec((1, d), lambda b, c: (0, 0))
    return pl.pallas_call(
        functools.partial(_hgrn_kernel, block=block, blocks_per_step=bps, n_steps=nt),
        grid=(batch, nt),
        in_specs=[col(2), col(3), col(4), col(5), state(), vec(), vec(),
                  pl.BlockSpec((block, block), lambda b, c: (0, 0)),
                  pl.BlockSpec(masks.shape, lambda b, c: (0, 0, 0))],
        out_specs=[pl.BlockSpec((tt, d), lambda b, c: (row(b, c), 0)), state()],
        out_shape=[jax.ShapeDtypeStruct((batch * seq, d), F32),
                   jax.ShapeDtypeStruct(s0.shape, F32)],
        scratch_shapes=[pltpu.VMEM((N_HGRN_HEADS, HGRN_HEAD_DIM, HGRN_HEAD_DIM), F32),
                        pltpu.VMEM((bps, block, d), F32)],
        compiler_params=_cparams("parallel", "arbitrary"),
        name="hgrn_branch",
    )(proj, proj, proj, proj, s0, lb.reshape(1, d), gn.reshape(1, d), tri, masks)


_ATT_PAIRS_PER_STEP = 8
_ATT_LANES = _ATT_PAIRS_PER_STEP * LANES
_ATT_STEPS = N_HEAD_PAIRS // _ATT_PAIRS_PER_STEP


def _attn_kernel(*refs, n_kv, dynamic_valid, tq):
    q_ref = refs[0]
    k_refs = refs[1:1 + n_kv]
    v_refs = refs[1 + n_kv:1 + 2 * n_kv]
    bias_ref = refs[1 + 2 * n_kv]
    o_ref = refs[2 + 2 * n_kv]

    lane = lax.broadcasted_iota(jnp.int32, (tq, LANES), 1)
    lo_half = lane < ATT_HEAD_DIM
    for pi in range(q_ref.shape[1] // LANES):
        ls = slice(pi * LANES, (pi + 1) * LANES)
        q = q_ref[:, ls]
        k = jnp.concatenate([r[:, ls] for r in k_refs], axis=0) if n_kv > 1 else k_refs[0][:, ls]
        v = jnp.concatenate([r[:, ls] for r in v_refs], axis=0) if n_kv > 1 else v_refs[0][:, ls]
        zero = jnp.zeros_like(q)
        outs = []
        for hh, qm in enumerate((jnp.where(lo_half, q, zero), jnp.where(lo_half, zero, q))):
            s = _dot_nt(qm, k) + bias_ref[2 * pi + hh]
            if dynamic_valid:
                first_valid = BAND_PAST - pl.program_id(2) * tq
                colj = lax.broadcasted_iota(jnp.int32, s.shape, 1)
                s = jnp.where(colj >= first_valid, s, MASK_VALUE)
            m = jnp.max(s, axis=-1, keepdims=True)
            p = jnp.exp(s - m)
            denom = jnp.sum(p, axis=-1, keepdims=True)
            outs.append(_dot(p.astype(BF16), v) / denom)
        o_ref[:, ls] = jnp.where(lo_half, outs[0], outs[1])


def _band_bias(table, tq, n_keys, banded):
    t = np.arange(tq)[:, None]
    j = np.arange(n_keys)[None, :]
    width = tq + n_keys - 1
    dist = BAND_PAST + (tq - 1) - np.arange(width + 1)
    diag = table[:, np.clip(dist, -MAX_REL, MAX_REL) + MAX_REL]
    skew = jnp.tile(diag, (1, tq))[:, :tq * width].reshape(-1, tq, width)
    bias = skew[:, :, tq - 1:tq - 1 + n_keys]
    if banded:
        c0 = (t // CHUNK) * CHUNK
        in_band = (j >= c0) & (j < c0 + BAND_PAST + CHUNK)
        bias = jnp.where(jnp.asarray(in_band)[None], bias, MASK_VALUE)
    return bias


def _attention_prompt(q16, k16, v16, table, batch, seq):
    tq = 256
    nt = seq // tq
    n_kv = BAND_PAST // tq + 1
    bias = _band_bias(table, tq, n_kv * tq, True)
    kv = lambda back: pl.BlockSpec(
        (tq, _ATT_LANES), lambda p, b, i: (b * nt + jnp.maximum(i - back, 0), p))
    kv_specs = [kv(n_kv - 1 - j) for j in range(n_kv)]
    return pl.pallas_call(
        functools.partial(_attn_kernel, n_kv=n_kv, dynamic_valid=True, tq=tq),
        grid=(_ATT_STEPS, batch, nt),
        in_specs=[pl.BlockSpec((tq, _ATT_LANES), lambda p, b, i: (b * nt + i, p))] + kv_specs + kv_specs
                 + [pl.BlockSpec((2 * _ATT_PAIRS_PER_STEP, tq, n_kv * tq), lambda p, b, i: (p, 0, 0))],
        out_specs=pl.BlockSpec((tq, _ATT_LANES), lambda p, b, i: (b * nt + i, p)),
        out_shape=jax.ShapeDtypeStruct((batch * seq, D_MODEL), F32),
        compiler_params=_cparams("parallel", "parallel", "arbitrary"),
        name="band_attention_prompt",
    )(q16, *([k16] * n_kv), *([v16] * n_kv), bias)


def _attention_sample(q16, k16, v16, cache_k16, cache_v16, table, batch, seq):
    cl = cache_k16.shape[0] // batch
    bias = _band_bias(table, seq, cl + seq, False)
    own = pl.BlockSpec((seq, _ATT_LANES), lambda p, b: (b, p))
    cache = pl.BlockSpec((cl, _ATT_LANES), lambda p, b: (b, p))
    return pl.pallas_call(
        functools.partial(_attn_kernel, n_kv=2, dynamic_valid=False, tq=seq),
        grid=(_ATT_STEPS, batch),
        in_specs=[own, cache, own, cache, own,
                  pl.BlockSpec((2 * _ATT_PAIRS_PER_STEP, seq, cl + seq), lambda p, b: (p, 0, 0))],
        out_specs=own,
        out_shape=jax.ShapeDtypeStruct((batch * seq, D_MODEL), F32),
        compiler_params=_cparams("parallel", "parallel"),
        name="band_attention_sample",
    )(q16, cache_k16, k16, cache_v16, v16, bias)


def _store_token_tiles(ref, x):
    rows = x.shape[0]
    for s in range(TOKEN_TILE_ROWS):
        ref[pl.ds(s, rows, stride=TOKEN_TILE_ROWS), :] = x[:, s * LANES:(s + 1) * LANES]


def _load_token_tiles(ref, rows, s):
    return ref[pl.ds(s, rows, stride=TOKEN_TILE_ROWS), :]


def _merge_kernel(ga_ref, gb_ref, gc_ref, ya_ref, yb_ref, yc_ref, x_ref, w_ref, gn_ref, *rest,
                  with_router):
    if with_router:
        wr_ref, x1_ref, xn_ref, route_ref = rest
    else:
        x1_ref, xn_ref = rest
    merged = (_sigmoid_tanh(ga_ref[...]) * ya_ref[...] + _sigmoid_tanh(gb_ref[...]) * yb_ref[...]
              + _sigmoid_tanh(gc_ref[...]) * yc_ref[...])
    x1 = x_ref[...] + _dot(merged.astype(BF16), w_ref[...])
    x1_ref[...] = x1
    ms = jnp.mean(x1 * x1, axis=-1, keepdims=True)
    xn = x1 * lax.rsqrt(ms + EPS) * gn_ref[...]
    if not with_router:
        xn_ref[...] = xn.astype(BF16)
    else:
        _store_token_tiles(xn_ref, xn)
        xh, xl, _ = _split3(xn)
        wh = wr_ref[0]
        wl = wr_ref[1]
        logits = _dot(xh, wh) + (_dot(xh, wl) + _dot(xl, wh))
        lane = lax.broadcasted_iota(jnp.int32, logits.shape, 1).astype(F32)
        neg = jnp.float32(-jnp.inf)
        logits = jnp.where(lane < N_EXPERTS, logits, neg)
        m1 = jnp.max(logits, axis=-1, keepdims=True)
        i1 = jnp.min(jnp.where(logits == m1, lane, float(LANES)), axis=-1, keepdims=True)
        rest_l = jnp.where(lane == i1, neg, logits)
        m2 = jnp.max(rest_l, axis=-1, keepdims=True)
        i2 = jnp.min(jnp.where(rest_l == m2, lane, float(LANES)), axis=-1, keepdims=True)
        e2 = jnp.exp(m2 - m1)
        p1 = 1.0 / (1.0 + e2)
        p2 = e2 / (1.0 + e2)
        route_ref[...] = jnp.where(lane == 0.0, i1, jnp.where(lane == 1.0, i2,
                                   jnp.where(lane == 2.0, p1, jnp.where(lane == 3.0, p2, 0.0))))


def _merge_out(proj, ya, yb, yc, x, w_out16, ffn_gain, router=None):
    n, d = x.shape
    tm = min(n, 256)
    col = lambda j: pl.BlockSpec((tm, d), lambda i: (i, j))
    rowspec = pl.BlockSpec((tm, d), lambda i: (i, 0))
    in_specs = [col(N_WIDE_BLOCKS - 3), col(N_WIDE_BLOCKS - 2), col(N_WIDE_BLOCKS - 1),
                rowspec, rowspec, rowspec, rowspec,
                pl.BlockSpec((d, d), lambda i: (0, 0)),
                pl.BlockSpec((1, d), lambda i: (0, 0))]
    args = [proj, proj, proj, ya, yb, yc, x, w_out16, ffn_gain.reshape(1, d)]
    out_specs = [rowspec, rowspec]
    out_shape = [jax.ShapeDtypeStruct((n, d), F32), jax.ShapeDtypeStruct((n, d), BF16)]
    if router is not None:
        out_specs[1] = pl.BlockSpec((tm * TOKEN_TILE_ROWS, LANES), lambda i: (i, 0))
        out_shape[1] = jax.ShapeDtypeStruct((n * TOKEN_TILE_ROWS, LANES), F32)
        wr = jnp.pad(router, ((0, 0), (0, LANES - N_EXPERTS)))
        wh = wr.astype(BF16)
        wl = (wr - wh.astype(F32)).astype(BF16)
        in_specs.append(pl.BlockSpec((2, d, LANES), lambda i: (0, 0, 0)))
        args.append(jnp.stack([wh, wl]))
        out_specs.append(pl.BlockSpec((tm, LANES), lambda i: (i, 0)))
        out_shape.append(jax.ShapeDtypeStruct((n, LANES), F32))
    return pl.pallas_call(
        functools.partial(_merge_kernel, with_router=router is not None),
        grid=(n // tm,),
        in_specs=in_specs,
        out_specs=out_specs,
        out_shape=out_shape,
        compiler_params=_cparams("parallel"),
        name="merge_out_proj",
    )(*args)


_FF_BLOCK = D_FF // 2
_N_FF_BLOCKS = D_FF // _FF_BLOCK


def _swiglu_block(xn16, wg, wu, wd):
    hg = _dot(xn16, wg)
    hu = _dot(xn16, wu)
    return _dot((_silu(hg) * hu).astype(BF16), wd)


def _ffn_kernel(xn_ref, x1_ref, wg_ref, wu_ref, wd_ref, o_ref, acc_ref):
    j = pl.program_id(1)
    y = _swiglu_block(xn_ref[...], wg_ref[...], wu_ref[...], wd_ref[...])

    @pl.when(j == 0)
    def _():
        acc_ref[...] = x1_ref[...] + y

    @pl.when(j > 0)
    def _():
        acc_ref[...] += y

    @pl.when(j == _N_FF_BLOCKS - 1)
    def _():
        o_ref[...] = acc_ref[...]


def _ffn(xn16, x1, wg16, wu16, wd16):
    n, d = x1.shape
    tm = min(n, 512)
    rowspec = lambda: pl.BlockSpec((tm, d), lambda i, j: (i, 0))
    return pl.pallas_call(
        _ffn_kernel,
        grid=(n // tm, _N_FF_BLOCKS),
        in_specs=[rowspec(), rowspec(),
                  pl.BlockSpec((d, _FF_BLOCK), lambda i, j: (0, j)),
                  pl.BlockSpec((d, _FF_BLOCK), lambda i, j: (0, j)),
                  pl.BlockSpec((_FF_BLOCK, d), lambda i, j: (j, 0))],
        out_specs=rowspec(),
        out_shape=jax.ShapeDtypeStruct((n, d), F32),
        scratch_shapes=[pltpu.VMEM((tm, d), F32)],
        compiler_params=_cparams("parallel", "arbitrary"),
        name="swiglu_ffn",
    )(xn16, x1, wg16, wu16, wd16)


_DISPATCH_CHUNK = 2048
_COMBINE_TOKENS = 256


def _routing_tables(route, n, tm):
    experts = route[:, :TOP_K].astype(jnp.int32).reshape(-1)
    onehot = (jnp.arange(N_EXPERTS, dtype=jnp.int32)[:, None] == experts[None]).astype(jnp.int32)
    running = jnp.cumsum(onehot, axis=1)
    rank = jnp.sum(running * onehot, axis=0) - 1
    count = running[:, -1]
    group = (count + tm - 1) // tm * tm
    group_end = jnp.cumsum(group)
    group_start = group_end - group
    dest = jnp.sum(onehot * group_start[:, None], axis=0) + rank
    n_tiles = TOP_K * n // tm + N_EXPERTS
    tile_start = jnp.arange(n_tiles, dtype=jnp.int32) * tm
    tile_expert = jnp.minimum(jnp.sum(tile_start[:, None] >= group_end[None], axis=1), N_EXPERTS - 1)
    n_used = (group_end[-1] // tm).reshape(1)
    return dest.astype(jnp.int32), tile_expert.astype(jnp.int32), n_used.astype(jnp.int32)


def _token_rows(index):
    return pl.ds(pl.multiple_of(index * TOKEN_TILE_ROWS, TOKEN_TILE_ROWS), TOKEN_TILE_ROWS)


def _dispatch_kernel(dest_ref, src_ref, init_ref, dst_ref, sem):
    del init_ref
    chunk = dest_ref.shape[2]

    def issue(token, carry):
        for k in range(TOP_K):
            pltpu.make_async_copy(src_ref.at[_token_rows(token)],
                                  dst_ref.at[_token_rows(dest_ref[0, 0, TOP_K * token + k])],
                                  sem).start()
        return carry

    lax.fori_loop(0, chunk // TOP_K, issue, 0)
    for k in range(TOP_K):
        pltpu.make_async_copy(src_ref, dst_ref.at[pl.ds(0, src_ref.shape[0])], sem).wait()


def _dispatch(xn_tiles, dest, n_slots):
    n_assign = dest.shape[0]
    chunk = min(n_assign, _DISPATCH_CHUNK)
    steps = n_assign // chunk
    shape = jax.ShapeDtypeStruct((n_slots * TOKEN_TILE_ROWS, LANES), F32)
    return pl.pallas_call(
        _dispatch_kernel,
        grid=(steps,),
        in_specs=[pl.BlockSpec((1, 1, chunk), lambda i: (i, 0, 0), memory_space=pltpu.SMEM),
                  pl.BlockSpec((chunk // TOP_K * TOKEN_TILE_ROWS, LANES), lambda i: (i, 0)),
                  pl.BlockSpec(memory_space=pl.ANY)],
        out_specs=pl.BlockSpec(memory_space=pl.ANY),
        out_shape=shape,
        scratch_shapes=[pltpu.SemaphoreType.DMA(())],
        input_output_aliases={2: 0},
        compiler_params=_cparams("arbitrary"),
        name="moe_dispatch",
    )(dest.reshape(steps, 1, chunk), xn_tiles, jnp.zeros(shape.shape, F32))


def _expert_ffn_kernel(te_ref, nu_ref, x_ref, wg_ref, wu_ref, wd_ref, y_ref, xs_ref, acc_ref):
    del te_ref
    i = pl.program_id(0)
    j = pl.program_id(1)
    tm = xs_ref.shape[0]
    used = i < nu_ref[0]

    @pl.when(used & (j == 0))
    def _():
        for s in range(TOKEN_TILE_ROWS):
            xs_ref[:, s * LANES:(s + 1) * LANES] = _load_token_tiles(x_ref, tm, s).astype(BF16)

    @pl.when(used)
    def _():
        y = _swiglu_block(xs_ref[...], wg_ref[0], wu_ref[0], wd_ref[0])

        @pl.when(j == 0)
        def _():
            acc_ref[...] = y

        @pl.when(j > 0)
        def _():
            acc_ref[...] += y

    @pl.when(used & (j == _N_FF_BLOCKS - 1))
    def _():
        _store_token_tiles(y_ref, acc_ref[...])

    @pl.when(jnp.logical_not(used) & (j == _N_FF_BLOCKS - 1))
    def _():
        y_ref[...] = jnp.zeros(y_ref.shape, F32)


def _expert_ffn(x_sorted, tile_expert, n_used, wg16, wu16, wd16, tm):
    d = D_MODEL
    n_tiles = tile_expert.shape[0]
    rows = pl.BlockSpec((tm * TOKEN_TILE_ROWS, LANES), lambda i, j, te, nu: (i, 0))
    grid_spec = pltpu.PrefetchScalarGridSpec(
        num_scalar_prefetch=2,
        grid=(n_tiles, _N_FF_BLOCKS),
        in_specs=[rows,
                  pl.BlockSpec((1, d, _FF_BLOCK), lambda i, j, te, nu: (te[i], 0, j)),
                  pl.BlockSpec((1, d, _FF_BLOCK), lambda i, j, te, nu: (te[i], 0, j)),
                  pl.BlockSpec((1, _FF_BLOCK, d), lambda i, j, te, nu: (te[i], j, 0))],
        out_specs=rows,
        scratch_shapes=[pltpu.VMEM((tm, d), BF16), pltpu.VMEM((tm, d), F32)])
    return pl.pallas_call(
        _expert_ffn_kernel,
        grid_spec=grid_spec,
        out_shape=jax.ShapeDtypeStruct(x_sorted.shape, F32),
        compiler_params=_cparams("arbitrary", "arbitrary"),
        name="moe_expert_ffn",
    )(tile_expert, n_used, x_sorted, wg16, wu16, wd16)


def _combine_kernel(dest_ref, next_dest_ref, x1_ref, route_ref, y_ref, o_ref, buf_ref, sem):
    i = pl.program_id(0)
    last = pl.num_programs(0) - 1
    tc = x1_ref.shape[0]

    def copies(idx_ref, slot, t, k):
        return pltpu.make_async_copy(y_ref.at[_token_rows(idx_ref[0, 0, TOP_K * t + k])],
                                     buf_ref.at[slot, k, _token_rows(t)], sem.at[slot])

    def issue(idx_ref, slot):
        def body(t, carry):
            for k in range(TOP_K):
                copies(idx_ref, slot, t, k).start()
            return carry
        lax.fori_loop(0, tc, body, 0)

    def finish(slot):
        for k in range(TOP_K):
            pltpu.make_async_copy(y_ref.at[pl.ds(0, tc * TOKEN_TILE_ROWS)], buf_ref.at[slot, k],
                                  sem.at[slot]).wait()
        route = route_ref[...]
        p1 = route[:, TOP_K:TOP_K + 1]
        p2 = route[:, TOP_K + 1:TOP_K + 2]
        for s in range(TOKEN_TILE_ROWS):
            sl = slice(s * LANES, (s + 1) * LANES)
            o_ref[:, sl] = (x1_ref[:, sl] + p1 * _load_token_tiles(buf_ref.at[slot, 0], tc, s)
                            + p2 * _load_token_tiles(buf_ref.at[slot, 1], tc, s))

    @pl.when(i == 0)
    def _():
        issue(dest_ref, 0)

    for slot in range(2):
        @pl.when(i % 2 == slot)
        def _(slot=slot):
            @pl.when(i < last)
            def _():
                issue(next_dest_ref, 1 - slot)
            finish(slot)


def _combine(x1, route, y_sorted, dest):
    n, d = x1.shape
    tc = min(n, _COMBINE_TOKENS)
    steps = n // tc
    dest3 = dest.reshape(steps, 1, TOP_K * tc)
    smem = lambda index_map: pl.BlockSpec((1, 1, TOP_K * tc), index_map, memory_space=pltpu.SMEM)
    return pl.pallas_call(
        _combine_kernel,
        grid=(steps,),
        in_specs=[smem(lambda i: (i, 0, 0)),
                  smem(lambda i: (jnp.minimum(i + 1, steps - 1), 0, 0)),
                  pl.BlockSpec((tc, d), lambda i: (i, 0)),
                  pl.BlockSpec((tc, LANES), lambda i: (i, 0)),
                  pl.BlockSpec(memory_space=pl.ANY)],
        out_specs=pl.BlockSpec((tc, d), lambda i: (i, 0)),
        out_shape=jax.ShapeDtypeStruct((n, d), F32),
        scratch_shapes=[pltpu.VMEM((2, TOP_K, tc * TOKEN_TILE_ROWS, LANES), F32),
                        pltpu.SemaphoreType.DMA((2,))],
        compiler_params=_cparams("arbitrary"),
        name="moe_combine",
    )(dest3, dest3, x1, route, y_sorted)


def _moe(xn_tiles, x1, route, wg16, wu16, wd16):
    n = x1.shape[0]
    tm = 512 if n >= 8192 else 256
    dest, tile_expert, n_used = _routing_tables(route, n, tm)
    x_sorted = _dispatch(xn_tiles, dest, tile_expert.shape[0] * tm)
    y_sorted = _expert_ffn(x_sorted, tile_expert, n_used, wg16, wu16, wd16, tm)
    return _combine(x1, route, y_sorted, dest)


def _mixer(x, layer_w, conv0, h0, s0, cache, batch, seq):
    keep = seq if cache is not None else min(BAND_PAST, seq)
    proj, q16, k16, v16, k_tail, v_tail = _in_proj(
        x, layer_w["norm_mix"], layer_w["w_in"], layer_w["q_norm"], layer_w["k_norm"], seq, keep)
    ya, conv_new, h_new = _lru_branch(proj, conv0, h0, layer_w["conv_w"], layer_w["conv_b"],
                                      layer_w["wri"], layer_w["lru_br"], layer_w["lru_bi"],
                                      layer_w["lru_lambda"], batch, seq)
    yb, s_new = _hgrn_branch(proj, s0, layer_w["lb"], layer_w["hgrn_norm"], batch, seq)
    if cache is None:
        yc = _attention_prompt(q16, k16, v16, layer_w["rel_table"], batch, seq)
    else:
        yc = _attention_sample(q16, k16, v16, cache[0], cache[1], layer_w["rel_table"], batch, seq)
    heads = (batch, keep, N_ATT_HEADS, ATT_HEAD_DIM)
    states = (conv_new, h_new.reshape(batch, D_MODEL), s_new, k_tail.reshape(heads), v_tail.reshape(heads))
    return proj, ya, yb, yc, states


def kernel(x_prompt, x_sample, state_conv, state_lru, state_hgrn, cache_k, cache_v, norm_mix, w_in, conv_w, conv_b, lru_wr, lru_br, lru_wi, lru_bi, lru_lambda, hgrn_gamma, hgrn_norm, q_norm, k_norm, rel_bias_table, w_out, norm_ffn, ffn_w_gate, ffn_w_up, ffn_w_down, moe_router, moe_w_gate, moe_w_up, moe_w_down):
    depth = w_in.shape[0]
    bp, sp, d = x_prompt.shape
    bs, ss, _ = x_sample.shape

    p = jax.nn.softmax(hgrn_gamma.astype(F32), axis=0)
    lb_all = jnp.cumsum(p, axis=0) - p

    xp = x_prompt.reshape(bp * sp, d)
    xs = x_sample.reshape(bs * ss, d)
    zero_conv = jnp.zeros((bp, CONV_W - 1, d), F32)
    zero_h = jnp.zeros((bp, d), F32)
    zero_s = jnp.zeros((bp,) + state_hgrn.shape[2:], F32)

    st_p, st_s = [], []
    for l in range(depth):
        layer_w = dict(
            norm_mix=norm_mix[l], w_in=w_in[l].astype(BF16), conv_w=conv_w[l], conv_b=conv_b[l],
            wri=jnp.concatenate([lru_wr[l], lru_wi[l]], axis=-1).astype(BF16),
            lru_br=lru_br[l], lru_bi=lru_bi[l], lru_lambda=lru_lambda[l], lb=lb_all[l],
            hgrn_norm=hgrn_norm[l], q_norm=q_norm[l], k_norm=k_norm[l],
            rel_table=rel_bias_table[l])
        w_out16 = w_out[l].astype(BF16)
        cl = cache_k.shape[2]
        cache = (cache_k[l].reshape(bs * cl, d).astype(BF16),
                 cache_v[l].reshape(bs * cl, d).astype(BF16))
        dense = l % 2 == 0
        if dense:
            wg16 = ffn_w_gate[l // 2].astype(BF16)
            wu16 = ffn_w_up[l // 2].astype(BF16)
            wd16 = ffn_w_down[l // 2].astype(BF16)
            router = None
        else:
            wg16 = moe_w_gate[l // 2].astype(BF16)
            wu16 = moe_w_up[l // 2].astype(BF16)
            wd16 = moe_w_down[l // 2].astype(BF16)
            router = moe_router[l // 2]

        new_x = []
        for x, conv0, h0, s0, cch, batch, seq, acc in (
                (xp, zero_conv, zero_h, zero_s, None, bp, sp, st_p),
                (xs, state_conv[l], state_lru[l], state_hgrn[l], cache, bs, ss, st_s)):
            proj, ya, yb, yc, states = _mixer(x, layer_w, conv0, h0, s0, cch, batch, seq)
            outs = _merge_out(proj, ya, yb, yc, x, w_out16, norm_ffn[l], router)
            if router is None:
                new_x.append(_ffn(outs[1], outs[0], wg16, wu16, wd16))
            else:
                new_x.append(_moe(outs[1], outs[0], outs[2], wg16, wu16, wd16))
            acc.append(states)
        xp, xs = new_x

    stack = lambda sts, i: jnp.stack([s[i] for s in sts])
    return (xp.reshape(bp, sp, d), xs.reshape(bs, ss, d),
            stack(st_p, 0), stack(st_p, 1), stack(st_p, 2), stack(st_p, 3), stack(st_p, 4),
            stack(st_s, 0), stack(st_s, 1), stack(st_s, 2), stack(st_s, 3), stack(st_s, 4))
```

```python
import functools

import numpy as np
import jax
import jax.numpy as jnp
from jax import lax
from jax.experimental import pallas as pl
from jax.experimental.pallas import tpu as pltpu

F32 = jnp.float32
BF16 = jnp.bfloat16

D_MODEL = 1024
CHUNK = 64
EPS = 1e-6
MIN_FORGET = 1e-20
N_LRU_BLOCKS = 8
LRU_BLOCK = D_MODEL // N_LRU_BLOCKS
CONV_W = 4
LRU_C = 8.0
HGRN_HEAD_DIM = 128
N_HGRN_HEADS = D_MODEL // HGRN_HEAD_DIM
ATT_HEAD_DIM = 64
N_ATT_HEADS = D_MODEL // ATT_HEAD_DIM
N_HEAD_PAIRS = N_ATT_HEADS // 2
ATT_SCALE = ATT_HEAD_DIM ** -0.5
PAST_CHUNKS = 8
BAND_PAST = PAST_CHUNKS * CHUNK
MAX_REL = 128
MASK_VALUE = -1e30
D_FF = 2816
N_EXPERTS = 8
TOP_K = 2
N_PROJ_BLOCKS = 12

LANES = 128
SUBLANES = 8
TOKEN_TILE_ROWS = D_MODEL // LANES
VMEM_LIMIT_BYTES = 56 * 1024 * 1024


def _cparams(*semantics):
    return pltpu.CompilerParams(dimension_semantics=semantics,
                                vmem_limit_bytes=VMEM_LIMIT_BYTES)


def _split3(x):
    hi = x.astype(BF16)
    r1 = x - hi.astype(F32)
    mid = r1.astype(BF16)
    lo = (r1 - mid.astype(F32)).astype(BF16)
    return hi, mid, lo


def _dot(a, b):
    return jnp.dot(a, b, preferred_element_type=F32)


def _dot_nt(a, b):
    return lax.dot_general(a, b, (((1,), (1,)), ((), ())), preferred_element_type=F32)


def _dot_tn(a, b):
    return lax.dot_general(a, b, (((0,), (0,)), ((), ())), preferred_element_type=F32)


def _sigmoid(x):
    return 1.0 / (1.0 + jnp.exp(-x))


def _sigmoid_tanh(x):
    return 0.5 * jnp.tanh(0.5 * x) + 0.5


def _silu(x):
    half = 0.5 * x
    return half * (1.0 + jnp.tanh(half))


_Q_COL, _K_COL, _V_COL = 6, 7, 8
N_WIDE_BLOCKS = N_PROJ_BLOCKS - 3


def _head_indicator():
    e = np.zeros((D_MODEL, LANES), np.float32)
    e[np.arange(D_MODEL), np.arange(D_MODEL) // ATT_HEAD_DIM] = 1.0
    return e


def _in_proj_kernel(x_ref, g_ref, w_ref, qg_ref, kg_ref, e_ref, et_ref,
                    o_ref, q16_ref, k16_ref, v16_ref, k32_ref, v32_ref, xn_ref,
                    *, tiles_per_tail, keep):
    i = pl.program_id(0)
    j = pl.program_id(1)
    tm = x_ref.shape[0]

    @pl.when(j == 0)
    def _():
        x = x_ref[...]
        ms = jnp.mean(x * x, axis=-1, keepdims=True)
        xn_ref[...] = (x * lax.rsqrt(ms + EPS) * g_ref[...]).astype(BF16)

    acc = lambda: _dot(xn_ref[...], w_ref[...])
    is_tail = (i % tiles_per_tail) == tiles_per_tail - 1

    def head_norm(x, gain):
        ms = _dot((x * x).astype(BF16), e_ref[...]) * (1.0 / ATT_HEAD_DIM)
        rhi, rmid, _ = _split3(lax.rsqrt(ms + EPS))
        return x * (_dot(rhi, et_ref[...]) + _dot(rmid, et_ref[...])) * gain

    @pl.when((j < _Q_COL) | (j > _V_COL))
    def _():
        o_ref[...] = acc()

    @pl.when(j == _Q_COL)
    def _():
        q16_ref[...] = (head_norm(acc(), qg_ref[...]) * ATT_SCALE).astype(BF16)

    @pl.when(j == _K_COL)
    def _():
        kn = head_norm(acc(), kg_ref[...])
        k16_ref[...] = kn.astype(BF16)

        @pl.when(is_tail)
        def _():
            k32_ref[...] = kn[tm - keep:, :]

    @pl.when(j == _V_COL)
    def _():
        v = acc()
        v16_ref[...] = v.astype(BF16)

        @pl.when(is_tail)
        def _():
            v32_ref[...] = v[tm - keep:, :]


def _in_proj(x, g, w, q_gain, k_gain, seq, keep):
    n, d = x.shape
    tm = min(n, 1024)
    tn = d
    if keep == seq:
        tiles_per_tail, keep_rows = 1, tm
    else:
        assert seq % tm == 0 and keep <= tm
        tiles_per_tail, keep_rows = seq // tm, keep
    n_tail = n // (tiles_per_tail * tm) * keep_rows
    e = _head_indicator()
    tile_gain = lambda gain: jnp.tile(gain, N_ATT_HEADS).reshape(1, d)
    wide_col = lambda j: jnp.where(j < _Q_COL, j, jnp.maximum(j - 3, _Q_COL - 1))

    def written_at(col, tiles):
        first = (tiles - 1) * N_PROJ_BLOCKS + col
        return lambda i, j: (jnp.maximum(i * N_PROJ_BLOCKS + j - first, 0) // (tiles * N_PROJ_BLOCKS), 0)

    row16 = lambda col: pl.BlockSpec((tm, d), written_at(col, 1))
    tail = lambda col: pl.BlockSpec((keep_rows, d), written_at(col, tiles_per_tail))
    vec = lambda: pl.BlockSpec((1, d), lambda i, j: (0, 0))
    return pl.pallas_call(
        functools.partial(_in_proj_kernel, tiles_per_tail=tiles_per_tail, keep=keep_rows),
        grid=(n // tm, N_PROJ_BLOCKS),
        in_specs=[
            pl.BlockSpec((tm, d), lambda i, j: (i, 0)),
            vec(),
            pl.BlockSpec((d, tn), lambda i, j: (0, j)),
            vec(), vec(),
            pl.BlockSpec((d, LANES), lambda i, j: (0, 0)),
            pl.BlockSpec((LANES, d), lambda i, j: (0, 0)),
        ],
        out_specs=[pl.BlockSpec((None, tm, tn), lambda i, j: (wide_col(j), i, 0)),
                   row16(_Q_COL), row16(_K_COL), row16(_V_COL), tail(_K_COL), tail(_V_COL)],
        out_shape=[jax.ShapeDtypeStruct((N_WIDE_BLOCKS, n, d), F32)]
                  + [jax.ShapeDtypeStruct((n, d), BF16)] * 3
                  + [jax.ShapeDtypeStruct((n_tail, d), F32)] * 2,
        scratch_shapes=[pltpu.VMEM((tm, d), BF16)],
        compiler_params=_cparams("arbitrary", "arbitrary"),
        name="norm_in_proj",
    )(x, g.reshape(1, d), w, tile_gain(q_gain), tile_gain(k_gain),
      jnp.asarray(e, BF16), jnp.asarray(e.T, BF16))


_XP_PAD = SUBLANES


def _lru_kernel(ax_ref, ag_ref, conv0_ref, h0_ref, cw_ref, cb_ref, wri_ref, br_ref, bi_ref,
                lam_ref, y_ref, conv_out_ref, h_out_ref, xp_ref, a_ref, u_ref, h_ref, hc_ref):
    i = pl.program_id(1)
    tt = ax_ref.shape[0]
    keep = CONV_W - 1

    @pl.when(i == 0)
    def _():
        xp_ref[_XP_PAD - keep:_XP_PAD, :] = conv0_ref[0]
        hc_ref[...] = h0_ref[0]

    x = ax_ref[...]
    xp_ref[_XP_PAD:_XP_PAD + tt, :] = x
    conv = cb_ref[...] + cw_ref[CONV_W - 1:CONV_W, :] * x
    for j in range(CONV_W - 1):
        off = _XP_PAD - (CONV_W - 1 - j)
        conv = conv + cw_ref[j:j + 1, :] * xp_ref[off:off + tt, :]
    tail = xp_ref[_XP_PAD + tt - keep:_XP_PAD + tt, :]
    xp_ref[_XP_PAD - keep:_XP_PAD, :] = tail
    conv_out_ref[0] = tail

    cb16 = conv.astype(BF16)
    r_parts, i_parts = [], []
    for n in range(N_LRU_BLOCKS):
        g = _dot(cb16[:, n * LRU_BLOCK:(n + 1) * LRU_BLOCK], wri_ref[n])
        r_parts.append(g[:, :LRU_BLOCK])
        i_parts.append(g[:, LRU_BLOCK:])
    r = _sigmoid_tanh(jnp.concatenate(r_parts, axis=-1) + br_ref[...])
    ig = _sigmoid_tanh(jnp.concatenate(i_parts, axis=-1) + bi_ref[...])
    lam = lam_ref[...]
    softplus_neg = jnp.maximum(-lam, 0.0) + jnp.log1p(jnp.exp(-jnp.abs(lam)))
    log_a = -LRU_C * r * softplus_neg
    a = jnp.exp(log_a)
    th = jnp.tanh(log_a)
    u = jnp.sqrt(jnp.maximum(-2.0 * th / (1.0 - th), 0.0)) * ig * conv

    row = lax.broadcasted_iota(jnp.int32, a.shape, 0) % SUBLANES
    d = 1
    while d < SUBLANES:
        ok = row >= d
        a_sh = jnp.where(ok, pltpu.roll(a, d, 0), 1.0)
        u_sh = jnp.where(ok, pltpu.roll(u, d, 0), 0.0)
        u = a * u_sh + u
        a = a * a_sh
        d *= 2
    a_ref[...] = a
    u_ref[...] = u

    def group(gi, h):
        r0 = pl.multiple_of(gi * SUBLANES, SUBLANES)
        hh = a_ref[pl.ds(r0, SUBLANES), :] * h + u_ref[pl.ds(r0, SUBLANES), :]
        h_ref[pl.ds(r0, SUBLANES), :] = hh
        return hh[SUBLANES - 1:SUBLANES, :]

    h_last = lax.fori_loop(0, tt // SUBLANES, group, hc_ref[...], unroll=4)
    hc_ref[...] = h_last
    h_out_ref[0] = h_last
    y_ref[...] = h_ref[...] * jax.nn.gelu(ag_ref[...])


def _lru_branch(proj, conv0, h0, conv_w, conv_b, wri, b_r, b_i, lam, batch, seq):
    d = D_MODEL
    tt = min(seq, 256)
    nt = seq // tt
    row = lambda b, i: b * nt + i
    vec = lambda: pl.BlockSpec((1, d), lambda b, i: (0, 0))
    return pl.pallas_call(
        _lru_kernel,
        grid=(batch, nt),
        in_specs=[
            pl.BlockSpec((None, tt, d), lambda b, i: (0, row(b, i), 0)),
            pl.BlockSpec((None, tt, d), lambda b, i: (1, row(b, i), 0)),
            pl.BlockSpec((1, CONV_W - 1, d), lambda b, i: (b, 0, 0)),
            pl.BlockSpec((1, 1, d), lambda b, i: (b, 0, 0)),
            pl.BlockSpec((CONV_W, d), lambda b, i: (0, 0)),
            vec(),
            pl.BlockSpec((N_LRU_BLOCKS, LRU_BLOCK, 2 * LRU_BLOCK), lambda b, i: (0, 0, 0)),
            vec(), vec(), vec(),
        ],
        out_specs=[
            pl.BlockSpec((tt, d), lambda b, i: (row(b, i), 0)),
            pl.BlockSpec((1, CONV_W - 1, d), lambda b, i: (b, 0, 0)),
            pl.BlockSpec((1, 1, d), lambda b, i: (b, 0, 0)),
        ],
        out_shape=[
            jax.ShapeDtypeStruct((batch * seq, d), F32),
            jax.ShapeDtypeStruct((batch, CONV_W - 1, d), F32),
            jax.ShapeDtypeStruct((batch, 1, d), F32),
        ],
        scratch_shapes=[
            pltpu.VMEM((_XP_PAD + tt, d), F32),
            pltpu.VMEM((tt, d), F32),
            pltpu.VMEM((tt, d), F32),
            pltpu.VMEM((tt, d), F32),
            pltpu.VMEM((1, d), F32),
        ],
        compiler_params=_cparams("parallel", "arbitrary"),
        name="lru_branch",
    )(proj, proj, conv0, h0.reshape(batch, 1, d), conv_w, conv_b.reshape(1, d), wri,
      b_r.reshape(1, d), b_i.reshape(1, d), lam.reshape(1, d))


_HGRN_BLOCK = 128
_HGRN_BLOCKS_PER_STEP = 4


def _hgrn_levels(block):
    return tuple(block >> (i + 1) for i in range(block.bit_length() - 1))


def _hgrn_masks(block):
    t = np.arange(block)[:, None]
    s = np.arange(block)[None, :]
    masks = []
    for g in _hgrn_levels(block):
        masks.append((t // (2 * g) == s // (2 * g)) & (t % (2 * g) >= g) & (s % (2 * g) < g))
    masks.append(t == s)
    return np.stack(masks).astype(np.float32)


def _hgrn_kernel(q_ref, f_ref, v_ref, g_ref, s0_ref, lb_ref, gn_ref, tri_ref, mask_ref,
                 y_ref, s_out_ref, st_ref, b_ref, *, block, blocks_per_step, n_steps):
    step = pl.program_id(1)

    def load_state():
        for h in range(N_HGRN_HEADS):
            st_ref[h] = s0_ref[0, h].T

    def store_state():
        for h in range(N_HGRN_HEADS):
            s_out_ref[0, h] = st_ref[h].T

    if n_steps == 1:
        load_state()
    else:
        pl.when(step == 0)(load_state)

    lb = lb_ref[...]
    gn = gn_ref[...]
    tri = tri_ref[...]
    rowi = lax.broadcasted_iota(jnp.int32, (block, D_MODEL), 0)

    for ci in range(blocks_per_step):
        rows = slice(ci * block, (ci + 1) * block)
        th = jnp.tanh(0.5 * f_ref[rows, :])
        f = lb + (1.0 - lb) * (0.5 + 0.5 * th)
        kb = (1.0 - lb) * (0.5 - 0.5 * th)
        qs = _silu(q_ref[rows, :])
        v16 = v_ref[rows, :].astype(BF16)

        hi, mid, lo = _split3(jnp.log2(jnp.maximum(f, MIN_FORGET)))
        b = _dot(tri, hi) + _dot(tri, mid) + _dot(tri, lo)
        bc_ref = b_ref.at[ci]
        bc_ref[...] = b
        b_last = bc_ref[block - 1:block, :]

        q_inter = (qs * jnp.exp2(b)).astype(BF16)
        k_state = (kb * jnp.exp2(b_last - b)).astype(BF16)

        q_lv, k_lv = [], []
        for g in _hgrn_levels(block):
            if g >= 4:
                blocks = [jnp.broadcast_to(bc_ref[m * 2 * g + g - 1:m * 2 * g + g, :], (2 * g, D_MODEL))
                          for m in range(block // (2 * g))]
                ref = jnp.concatenate(blocks, axis=0) if len(blocks) > 1 else blocks[0]
            elif g == 2:
                p = rowi % 4
                ref = jnp.where(p == 0, pltpu.roll(b, block - 1, 0),
                                jnp.where(p == 1, b, jnp.where(p == 2, pltpu.roll(b, 1, 0),
                                                               pltpu.roll(b, 2, 0))))
            else:
                ref = jnp.where(rowi % 2 == 0, b, pltpu.roll(b, 1, 0))
            fac = jnp.exp2(-jnp.abs(b - ref))
            q_lv.append((qs * fac).astype(BF16))
            k_lv.append((kb * fac).astype(BF16))
        q_lv.append(qs.astype(BF16))
        k_lv.append(kb.astype(BF16))

        gate = _silu(g_ref[rows, :])
        for h in range(N_HGRN_HEADS):
            sl = slice(h * HGRN_HEAD_DIM, (h + 1) * HGRN_HEAD_DIM)
            st = st_ref[h]
            att = mask_ref[0] * _dot_nt(q_lv[0][:, sl], k_lv[0][:, sl])
            for l in range(1, len(q_lv)):
                att = att + mask_ref[l] * _dot_nt(q_lv[l][:, sl], k_lv[l][:, sl])
            o = _dot_nt(q_inter[:, sl], st.astype(BF16)) + _dot(att.astype(BF16), v16[:, sl])
            ms = jnp.mean(o * o, axis=-1, keepdims=True)
            y_ref[rows, sl] = o * lax.rsqrt(ms + EPS) * gn[:, sl] * gate[:, sl]
            st_ref[h] = st * jnp.exp2(b_last[:, sl]) + _dot_tn(v16[:, sl], k_state[:, sl])

    if n_steps == 1:
        store_state()
    else:
        pl.when(step == n_steps - 1)(store_state)


def _hgrn_branch(proj, s0, lb, gn, batch, seq):
    d = D_MODEL
    block = min(seq, _HGRN_BLOCK)
    bps = min(seq // block, _HGRN_BLOCKS_PER_STEP)
    tt = bps * block
    nt = seq // tt
    row = lambda b, c: b * nt + c
    tri = jnp.asarray(np.tril(np.ones((block, block), np.float32)), BF16)
    masks = jnp.asarray(_hgrn_masks(block))
    col = lambda j: pl.BlockSpec((None, tt, d), lambda b, c: (j, row(b, c), 0))
    state = lambda: pl.BlockSpec((1, N_HGRN_HEADS, HGRN_HEAD_DIM, HGRN_HEAD_DIM),
                                 lambda b, c: (b, 0, 0, 0))
    vec = lambda: pl.BlockSpec((1, d), lambda b, c: (0, 0))
    return pl.pallas_call(
        functools.partial(_hgrn_kernel, block=block, blocks_per_step=bps, n_steps=nt),
        grid=(batch, nt),
        in_specs=[col(2), col(3), col(4), col(5), state(), vec(), vec(),
                  pl.BlockSpec((block, block), lambda b, c: (0, 0)),
                  pl.BlockSpec(masks.shape, lambda b, c: (0, 0, 0))],
        out_specs=[pl.BlockSpec((tt, d), lambda b, c: (row(b, c), 0)), state()],
        out_shape=[jax.ShapeDtypeStruct((batch * seq, d), F32),
                   jax.ShapeDtypeStruct(s0.shape, F32)],
        scratch_shapes=[pltpu.VMEM((N_HGRN_HEADS, HGRN_HEAD_DIM, HGRN_HEAD_DIM), F32),
                        pltpu.VMEM((bps, block, d), F32)],
        compiler_params=_cparams("parallel", "arbitrary"),
        name="hgrn_branch",
    )(proj, proj, proj, proj, s0, lb.reshape(1, d), gn.reshape(1, d), tri, masks)


_ATT_PAIRS_PER_STEP = 8
_ATT_LANES = _ATT_PAIRS_PER_STEP * LANES
_ATT_STEPS = N_HEAD_PAIRS // _ATT_PAIRS_PER_STEP


def _attn_kernel(*refs, n_kv, dynamic_valid, tq):
    q_ref = refs[0]
    k_refs = refs[1:1 + n_kv]
    v_refs = refs[1 + n_kv:1 + 2 * n_kv]
    bias_ref = refs[1 + 2 * n_kv]
    o_ref = refs[2 + 2 * n_kv]

    lane = lax.broadcasted_iota(jnp.int32, (tq, LANES), 1)
    lo_half = lane < ATT_HEAD_DIM
    for pi in range(q_ref.shape[1] // LANES):
        ls = slice(pi * LANES, (pi + 1) * LANES)
        q = q_ref[:, ls]
        k = jnp.concatenate([r[:, ls] for r in k_refs], axis=0) if n_kv > 1 else k_refs[0][:, ls]
        v = jnp.concatenate([r[:, ls] for r in v_refs], axis=0) if n_kv > 1 else v_refs[0][:, ls]
        zero = jnp.zeros_like(q)
        outs = []
        for hh, qm in enumerate((jnp.where(lo_half, q, zero), jnp.where(lo_half, zero, q))):
            s = _dot_nt(qm, k) + bias_ref[2 * pi + hh]
            if dynamic_valid:
                first_valid = BAND_PAST - pl.program_id(2) * tq
                colj = lax.broadcasted_iota(jnp.int32, s.shape, 1)
                s = jnp.where(colj >= first_valid, s, MASK_VALUE)
            m = jnp.max(s, axis=-1, keepdims=True)
            p = jnp.exp(s - m)
            denom = jnp.sum(p, axis=-1, keepdims=True)
            outs.append(_dot(p.astype(BF16), v) / denom)
        o_ref[:, ls] = jnp.where(lo_half, outs[0], outs[1])


def _band_bias(table, tq, n_keys, banded):
    t = np.arange(tq)[:, None]
    j = np.arange(n_keys)[None, :]
    width = tq + n_keys - 1
    dist = BAND_PAST + (tq - 1) - np.arange(width + 1)
    diag = table[:, np.clip(dist, -MAX_REL, MAX_REL) + MAX_REL]
    skew = jnp.tile(diag, (1, tq))[:, :tq * width].reshape(-1, tq, width)
    bias = skew[:, :, tq - 1:tq - 1 + n_keys]
    if banded:
        c0 = (t // CHUNK) * CHUNK
        in_band = (j >= c0) & (j < c0 + BAND_PAST + CHUNK)
        bias = jnp.where(jnp.asarray(in_band)[None], bias, MASK_VALUE)
    return bias


def _attention_prompt(q16, k16, v16, table, batch, seq):
    tq = 256
    nt = seq // tq
    n_kv = BAND_PAST // tq + 1
    bias = _band_bias(table, tq, n_kv * tq, True)
    kv = lambda back: pl.BlockSpec(
        (tq, _ATT_LANES), lambda p, b, i: (b * nt + jnp.maximum(i - back, 0), p))
    kv_specs = [kv(n_kv - 1 - j) for j in range(n_kv)]
    return pl.pallas_call(
        functools.partial(_attn_kernel, n_kv=n_kv, dynamic_valid=True, tq=tq),
        grid=(_ATT_STEPS, batch, nt),
        in_specs=[pl.BlockSpec((tq, _ATT_LANES), lambda p, b, i: (b * nt + i, p))] + kv_specs + kv_specs
                 + [pl.BlockSpec((2 * _ATT_PAIRS_PER_STEP, tq, n_kv * tq), lambda p, b, i: (p, 0, 0))],
        out_specs=pl.BlockSpec((tq, _ATT_LANES), lambda p, b, i: (b * nt + i, p)),
        out_shape=jax.ShapeDtypeStruct((batch * seq, D_MODEL), F32),
        compiler_params=_cparams("parallel", "parallel", "arbitrary"),
        name="band_attention_prompt",
    )(q16, *([k16] * n_kv), *([v16] * n_kv), bias)


def _attention_sample(q16, k16, v16, cache_k16, cache_v16, table, batch, seq):
    cl = cache_k16.shape[0] // batch
    bias = _band_bias(table, seq, cl + seq, False)
    own = pl.BlockSpec((seq, _ATT_LANES), lambda p, b: (b, p))
    cache = pl.BlockSpec((cl, _ATT_LANES), lambda p, b: (b, p))
    return pl.pallas_call(
        functools.partial(_attn_kernel, n_kv=2, dynamic_valid=False, tq=seq),
        grid=(_ATT_STEPS, batch),
        in_specs=[own, cache, own, cache, own,
                  pl.BlockSpec((2 * _ATT_PAIRS_PER_STEP, seq, cl + seq), lambda p, b: (p, 0, 0))],
        out_specs=own,
        out_shape=jax.ShapeDtypeStruct((batch * seq, D_MODEL), F32),
        compiler_params=_cparams("parallel", "parallel"),
        name="band_attention_sample",
    )(q16, cache_k16, k16, cache_v16, v16, bias)


def _store_token_tiles(ref, x):
    rows = x.shape[0]
    for s in range(TOKEN_TILE_ROWS):
        ref[pl.ds(s, rows, stride=TOKEN_TILE_ROWS), :] = x[:, s * LANES:(s + 1) * LANES]


def _load_token_tiles(ref, rows, s):
    return ref[pl.ds(s, rows, stride=TOKEN_TILE_ROWS), :]


def _merge_kernel(ga_ref, gb_ref, gc_ref, ya_ref, yb_ref, yc_ref, x_ref, w_ref, gn_ref, *rest,
                  with_router):
    if with_router:
        wr_ref, x1_ref, xn_ref, route_ref = rest
    else:
        x1_ref, xn_ref = rest
    merged = (_sigmoid_tanh(ga_ref[...]) * ya_ref[...] + _sigmoid_tanh(gb_ref[...]) * yb_ref[...]
              + _sigmoid_tanh(gc_ref[...]) * yc_ref[...])
    x1 = x_ref[...] + _dot(merged.astype(BF16), w_ref[...])
    x1_ref[...] = x1
    ms = jnp.mean(x1 * x1, axis=-1, keepdims=True)
    xn = x1 * lax.rsqrt(ms + EPS) * gn_ref[...]
    if not with_router:
        xn_ref[...] = xn.astype(BF16)
    else:
        _store_token_tiles(xn_ref, xn)
        xh, xl, _ = _split3(xn)
        wh = wr_ref[0]
        wl = wr_ref[1]
        logits = _dot(xh, wh) + (_dot(xh, wl) + _dot(xl, wh))
        lane = lax.broadcasted_iota(jnp.int32, logits.shape, 1).astype(F32)
        neg = jnp.float32(-jnp.inf)
        logits = jnp.where(lane < N_EXPERTS, logits, neg)
        m1 = jnp.max(logits, axis=-1, keepdims=True)
        i1 = jnp.min(jnp.where(logits == m1, lane, float(LANES)), axis=-1, keepdims=True)
        rest_l = jnp.where(lane == i1, neg, logits)
        m2 = jnp.max(rest_l, axis=-1, keepdims=True)
        i2 = jnp.min(jnp.where(rest_l == m2, lane, float(LANES)), axis=-1, keepdims=True)
        e2 = jnp.exp(m2 - m1)
        p1 = 1.0 / (1.0 + e2)
        p2 = e2 / (1.0 + e2)
        route_ref[...] = jnp.where(lane == 0.0, i1, jnp.where(lane == 1.0, i2,
                                   jnp.where(lane == 2.0, p1, jnp.where(lane == 3.0, p2, 0.0))))


def _merge_out(proj, ya, yb, yc, x, w_out16, ffn_gain, router=None):
    n, d = x.shape
    tm = min(n, 256)
    col = lambda j: pl.BlockSpec((None, tm, d), lambda i: (j, i, 0))
    rowspec = pl.BlockSpec((tm, d), lambda i: (i, 0))
    in_specs = [col(N_WIDE_BLOCKS - 3), col(N_WIDE_BLOCKS - 2), col(N_WIDE_BLOCKS - 1),
                rowspec, rowspec, rowspec, rowspec,
                pl.BlockSpec((d, d), lambda i: (0, 0)),
                pl.BlockSpec((1, d), lambda i: (0, 0))]
    args = [proj, proj, proj, ya, yb, yc, x, w_out16, ffn_gain.reshape(1, d)]
    out_specs = [rowspec, rowspec]
    out_shape = [jax.ShapeDtypeStruct((n, d), F32), jax.ShapeDtypeStruct((n, d), BF16)]
    if router is not None:
        out_specs[1] = pl.BlockSpec((tm * TOKEN_TILE_ROWS, LANES), lambda i: (i, 0))
        out_shape[1] = jax.ShapeDtypeStruct((n * TOKEN_TILE_ROWS, LANES), F32)
        wr = jnp.pad(router, ((0, 0), (0, LANES - N_EXPERTS)))
        wh = wr.astype(BF16)
        wl = (wr - wh.astype(F32)).astype(BF16)
        in_specs.append(pl.BlockSpec((2, d, LANES), lambda i: (0, 0, 0)))
        args.append(jnp.stack([wh, wl]))
        out_specs.append(pl.BlockSpec((tm, LANES), lambda i: (i, 0)))
        out_shape.append(jax.ShapeDtypeStruct((n, LANES), F32))
    return pl.pallas_call(
        functools.partial(_merge_kernel, with_router=router is not None),
        grid=(n // tm,),
        in_specs=in_specs,
        out_specs=out_specs,
        out_shape=out_shape,
        compiler_params=_cparams("parallel"),
        name="merge_out_proj",
    )(*args)


_FF_BLOCK = D_FF // 2
_N_FF_BLOCKS = D_FF // _FF_BLOCK


def _swiglu_block(xn16, wg, wu, wd):
    hg = _dot(xn16, wg)
    hu = _dot(xn16, wu)
    return _dot((_silu(hg) * hu).astype(BF16), wd)


def _ffn_kernel(xn_ref, x1_ref, wg_ref, wu_ref, wd_ref, o_ref, acc_ref):
    j = pl.program_id(1)
    y = _swiglu_block(xn_ref[...], wg_ref[...], wu_ref[...], wd_ref[...])

    @pl.when(j == 0)
    def _():
        acc_ref[...] = x1_ref[...] + y

    @pl.when(j > 0)
    def _():
        acc_ref[...] += y

    @pl.when(j == _N_FF_BLOCKS - 1)
    def _():
        o_ref[...] = acc_ref[...]


def _ffn(xn16, x1, wg16, wu16, wd16):
    n, d = x1.shape
    tm = min(n, 512)
    rowspec = lambda: pl.BlockSpec((tm, d), lambda i, j: (i, 0))
    return pl.pallas_call(
        _ffn_kernel,
        grid=(n // tm, _N_FF_BLOCKS),
        in_specs=[rowspec(), rowspec(),
                  pl.BlockSpec((d, _FF_BLOCK), lambda i, j: (0, j)),
                  pl.BlockSpec((d, _FF_BLOCK), lambda i, j: (0, j)),
                  pl.BlockSpec((_FF_BLOCK, d), lambda i, j: (j, 0))],
        out_specs=rowspec(),
        out_shape=jax.ShapeDtypeStruct((n, d), F32),
        scratch_shapes=[pltpu.VMEM((tm, d), F32)],
        compiler_params=_cparams("parallel", "arbitrary"),
        name="swiglu_ffn",
    )(xn16, x1, wg16, wu16, wd16)


_DISPATCH_CHUNK = 2048
_COMBINE_TOKENS = 256


def _routing_tables(route, n, tm):
    experts = route[:, :TOP_K].astype(jnp.int32).reshape(-1)
    onehot = (jnp.arange(N_EXPERTS, dtype=jnp.int32)[:, None] == experts[None]).astype(jnp.int32)
    running = jnp.cumsum(onehot, axis=1)
    rank = jnp.sum(running * onehot, axis=0) - 1
    count = running[:, -1]
    group = (count + tm - 1) // tm * tm
    group_end = jnp.cumsum(group)
    group_start = group_end - group
    dest = jnp.sum(onehot * group_start[:, None], axis=0) + rank
    n_tiles = TOP_K * n // tm + N_EXPERTS
    tile_start = jnp.arange(n_tiles, dtype=jnp.int32) * tm
    tile_expert = jnp.minimum(jnp.sum(tile_start[:, None] >= group_end[None], axis=1), N_EXPERTS - 1)
    n_used = (group_end[-1] // tm).reshape(1)
    return dest.astype(jnp.int32), tile_expert.astype(jnp.int32), n_used.astype(jnp.int32)


def _token_rows(index):
    return pl.ds(pl.multiple_of(index * TOKEN_TILE_ROWS, TOKEN_TILE_ROWS), TOKEN_TILE_ROWS)


def _dispatch_kernel(dest_ref, src_ref, init_ref, dst_ref, sem):
    del init_ref
    chunk = dest_ref.shape[2]

    def issue(token, carry):
        for k in range(TOP_K):
            pltpu.make_async_copy(src_ref.at[_token_rows(token)],
                                  dst_ref.at[_token_rows(dest_ref[0, 0, TOP_K * token + k])],
                                  sem).start()
        return carry

    lax.fori_loop(0, chunk // TOP_K, issue, 0)
    for k in range(TOP_K):
        pltpu.make_async_copy(src_ref, dst_ref.at[pl.ds(0, src_ref.shape[0])], sem).wait()


def _dispatch(xn_tiles, dest, n_slots):
    n_assign = dest.shape[0]
    chunk = min(n_assign, _DISPATCH_CHUNK)
    steps = n_assign // chunk
    shape = jax.ShapeDtypeStruct((n_slots * TOKEN_TILE_ROWS, LANES), F32)
    return pl.pallas_call(
        _dispatch_kernel,
        grid=(steps,),
        in_specs=[pl.BlockSpec((1, 1, chunk), lambda i: (i, 0, 0), memory_space=pltpu.SMEM),
                  pl.BlockSpec((chunk // TOP_K * TOKEN_TILE_ROWS, LANES), lambda i: (i, 0)),
                  pl.BlockSpec(memory_space=pl.ANY)],
        out_specs=pl.BlockSpec(memory_space=pl.ANY),
        out_shape=shape,
        scratch_shapes=[pltpu.SemaphoreType.DMA(())],
        input_output_aliases={2: 0},
        compiler_params=_cparams("arbitrary"),
        name="moe_dispatch",
    )(dest.reshape(steps, 1, chunk), xn_tiles, jnp.zeros(shape.shape, F32))


def _expert_ffn_kernel(te_ref, nu_ref, x_ref, wg_ref, wu_ref, wd_ref, y_ref, xs_ref, acc_ref):
    del te_ref
    i = pl.program_id(0)
    j = pl.program_id(1)
    tm = xs_ref.shape[0]
    used = i < nu_ref[0]

    @pl.when(used & (j == 0))
    def _():
        for s in range(TOKEN_TILE_ROWS):
            xs_ref[:, s * LANES:(s + 1) * LANES] = _load_token_tiles(x_ref, tm, s).astype(BF16)

    @pl.when(used)
    def _():
        y = _swiglu_block(xs_ref[...], wg_ref[0], wu_ref[0], wd_ref[0])

        @pl.when(j == 0)
        def _():
            acc_ref[...] = y

        @pl.when(j > 0)
        def _():
            acc_ref[...] += y

    @pl.when(used & (j == _N_FF_BLOCKS - 1))
    def _():
        _store_token_tiles(y_ref, acc_ref[...])

    @pl.when(jnp.logical_not(used) & (j == _N_FF_BLOCKS - 1))
    def _():
        y_ref[...] = jnp.zeros(y_ref.shape, F32)


def _expert_ffn(x_sorted, tile_expert, n_used, wg16, wu16, wd16, tm):
    d = D_MODEL
    n_tiles = tile_expert.shape[0]
    rows = pl.BlockSpec((tm * TOKEN_TILE_ROWS, LANES), lambda i, j, te, nu: (i, 0))
    grid_spec = pltpu.PrefetchScalarGridSpec(
        num_scalar_prefetch=2,
        grid=(n_tiles, _N_FF_BLOCKS),
        in_specs=[rows,
                  pl.BlockSpec((1, d, _FF_BLOCK), lambda i, j, te, nu: (te[i], 0, j)),
                  pl.BlockSpec((1, d, _FF_BLOCK), lambda i, j, te, nu: (te[i], 0, j)),
                  pl.BlockSpec((1, _FF_BLOCK, d), lambda i, j, te, nu: (te[i], j, 0))],
        out_specs=rows,
        scratch_shapes=[pltpu.VMEM((tm, d), BF16), pltpu.VMEM((tm, d), F32)])
    return pl.pallas_call(
        _expert_ffn_kernel,
        grid_spec=grid_spec,
        out_shape=jax.ShapeDtypeStruct(x_sorted.shape, F32),
        compiler_params=_cparams("arbitrary", "arbitrary"),
        name="moe_expert_ffn",
    )(tile_expert, n_used, x_sorted, wg16, wu16, wd16)


def _combine_kernel(dest_ref, next_dest_ref, x1_ref, route_ref, y_ref, o_ref, buf_ref, sem):
    i = pl.program_id(0)
    last = pl.num_programs(0) - 1
    tc = x1_ref.shape[0]

    def copies(idx_ref, slot, t, k):
        return pltpu.make_async_copy(y_ref.at[_token_rows(idx_ref[0, 0, TOP_K * t + k])],
                                     buf_ref.at[slot, k, _token_rows(t)], sem.at[slot])

    def issue(idx_ref, slot):
        def body(t, carry):
            for k in range(TOP_K):
                copies(idx_ref, slot, t, k).start()
            return carry
        lax.fori_loop(0, tc, body, 0)

    def finish(slot):
        for k in range(TOP_K):
            pltpu.make_async_copy(y_ref.at[pl.ds(0, tc * TOKEN_TILE_ROWS)], buf_ref.at[slot, k],
                                  sem.at[slot]).wait()
        route = route_ref[...]
        p1 = route[:, TOP_K:TOP_K + 1]
        p2 = route[:, TOP_K + 1:TOP_K + 2]
        for s in range(TOKEN_TILE_ROWS):
            sl = slice(s * LANES, (s + 1) * LANES)
            o_ref[:, sl] = (x1_ref[:, sl] + p1 * _load_token_tiles(buf_ref.at[slot, 0], tc, s)
                            + p2 * _load_token_tiles(buf_ref.at[slot, 1], tc, s))

    @pl.when(i == 0)
    def _():
        issue(dest_ref, 0)

    for slot in range(2):
        @pl.when(i % 2 == slot)
        def _(slot=slot):
            @pl.when(i < last)
            def _():
                issue(next_dest_ref, 1 - slot)
            finish(slot)


def _combine(x1, route, y_sorted, dest):
    n, d = x1.shape
    tc = min(n, _COMBINE_TOKENS)
    steps = n // tc
    dest3 = dest.reshape(steps, 1, TOP_K * tc)
    smem = lambda index_map: pl.BlockSpec((1, 1, TOP_K * tc), index_map, memory_space=pltpu.SMEM)
    return pl.pallas_call(
        _combine_kernel,
        grid=(steps,),
        in_specs=[smem(lambda i: (i, 0, 0)),
                  smem(lambda i: (jnp.minimum(i + 1, steps - 1), 0, 0)),
                  pl.BlockSpec((tc, d), lambda i: (i, 0)),
                  pl.BlockSpec((tc, LANES), lambda i: (i, 0)),
                  pl.BlockSpec(memory_space=pl.ANY)],
        out_specs=pl.BlockSpec((tc, d), lambda i: (i, 0)),
        out_shape=jax.ShapeDtypeStruct((n, d), F32),
        scratch_shapes=[pltpu.VMEM((2, TOP_K, tc * TOKEN_TILE_ROWS, LANES), F32),
                        pltpu.SemaphoreType.DMA((2,))],
        compiler_params=_cparams("arbitrary"),
        name="moe_combine",
    )(dest3, dest3, x1, route, y_sorted)


def _moe(xn_tiles, x1, route, wg16, wu16, wd16):
    n = x1.shape[0]
    tm = 512 if n >= 8192 else 256
    dest, tile_expert, n_used = _routing_tables(route, n, tm)
    x_sorted = _dispatch(xn_tiles, dest, tile_expert.shape[0] * tm)
    y_sorted = _expert_ffn(x_sorted, tile_expert, n_used, wg16, wu16, wd16, tm)
    return _combine(x1, route, y_sorted, dest)


def _mixer(x, layer_w, conv0, h0, s0, cache, batch, seq):
    keep = seq if cache is not None else min(BAND_PAST, seq)
    proj, q16, k16, v16, k_tail, v_tail = _in_proj(
        x, layer_w["norm_mix"], layer_w["w_in"], layer_w["q_norm"], layer_w["k_norm"], seq, keep)
    ya, conv_new, h_new = _lru_branch(proj, conv0, h0, layer_w["conv_w"], layer_w["conv_b"],
                                      layer_w["wri"], layer_w["lru_br"], layer_w["lru_bi"],
                                      layer_w["lru_lambda"], batch, seq)
    yb, s_new = _hgrn_branch(proj, s0, layer_w["lb"], layer_w["hgrn_norm"], batch, seq)
    if cache is None:
        yc = _attention_prompt(q16, k16, v16, layer_w["rel_table"], batch, seq)
    else:
        yc = _attention_sample(q16, k16, v16, cache[0], cache[1], layer_w["rel_table"], batch, seq)
    heads = (batch, keep, N_ATT_HEADS, ATT_HEAD_DIM)
    states = (conv_new, h_new.reshape(batch, D_MODEL), s_new, k_tail.reshape(heads), v_tail.reshape(heads))
    return proj, ya, yb, yc, states


def kernel(x_prompt, x_sample, state_conv, state_lru, state_hgrn, cache_k, cache_v, norm_mix, w_in, conv_w, conv_b, lru_wr, lru_br, lru_wi, lru_bi, lru_lambda, hgrn_gamma, hgrn_norm, q_norm, k_norm, rel_bias_table, w_out, norm_ffn, ffn_w_gate, ffn_w_up, ffn_w_down, moe_router, moe_w_gate, moe_w_up, moe_w_down):
    depth = w_in.shape[0]
    bp, sp, d = x_prompt.shape
    bs, ss, _ = x_sample.shape

    p = jax.nn.softmax(hgrn_gamma.astype(F32), axis=0)
    lb_all = jnp.cumsum(p, axis=0) - p

    xp = x_prompt.reshape(bp * sp, d)
    xs = x_sample.reshape(bs * ss, d)
    zero_conv = jnp.zeros((bp, CONV_W - 1, d), F32)
    zero_h = jnp.zeros((bp, d), F32)
    zero_s = jnp.zeros((bp,) + state_hgrn.shape[2:], F32)

    st_p, st_s = [], []
    for l in range(depth):
        layer_w = dict(
            norm_mix=norm_mix[l], w_in=w_in[l].astype(BF16), conv_w=conv_w[l], conv_b=conv_b[l],
            wri=jnp.concatenate([lru_wr[l], lru_wi[l]], axis=-1).astype(BF16),
            lru_br=lru_br[l], lru_bi=lru_bi[l], lru_lambda=lru_lambda[l], lb=lb_all[l],
            hgrn_norm=hgrn_norm[l], q_norm=q_norm[l], k_norm=k_norm[l],
            rel_table=rel_bias_table[l])
        w_out16 = w_out[l].astype(BF16)
        cl = cache_k.shape[2]
        cache = (cache_k[l].reshape(bs * cl, d).astype(BF16),
                 cache_v[l].reshape(bs * cl, d).astype(BF16))
        dense = l % 2 == 0
        if dense:
            wg16 = ffn_w_gate[l // 2].astype(BF16)
            wu16 = ffn_w_up[l // 2].astype(BF16)
            wd16 = ffn_w_down[l // 2].astype(BF16)
            router = None
        else:
            wg16 = moe_w_gate[l // 2].astype(BF16)
            wu16 = moe_w_up[l // 2].astype(BF16)
            wd16 = moe_w_down[l // 2].astype(BF16)
            router = moe_router[l // 2]

        new_x = []
        for x, conv0, h0, s0, cch, batch, seq, acc in (
                (xp, zero_conv, zero_h, zero_s, None, bp, sp, st_p),
                (xs, state_conv[l], state_lru[l], state_hgrn[l], cache, bs, ss, st_s)):
            proj, ya, yb, yc, states = _mixer(x, layer_w, conv0, h0, s0, cch, batch, seq)
            outs = _merge_out(proj, ya, yb, yc, x, w_out16, norm_ffn[l], router)
            if router is None:
                new_x.append(_ffn(outs[1], outs[0], wg16, wu16, wd16))
            else:
                new_x.append(_moe(outs[1], outs[0], outs[2], wg16, wu16, wd16))
            acc.append(states)
        xp, xs = new_x

    stack = lambda sts, i: jnp.stack([s[i] for s in sts])
    return (xp.reshape(bp, sp, d), xs.reshape(bs, ss, d),
            stack(st_p, 0), stack(st_p, 1), stack(st_p, 2), stack(st_p, 3), stack(st_p, 4),
            stack(st_s, 0), stack(st_s, 1), stack(st_s, 2), stack(st_s, 3), stack(st_s, 4))
```

```python
import functools

import numpy as np
import jax
import jax.numpy as jnp
from jax import lax
from jax.experimental import pallas as pl
from jax.experimental.pallas import tpu as pltpu

F32 = jnp.float32
BF16 = jnp.bfloat16

D_MODEL = 1024
CHUNK = 64
EPS = 1e-6
MIN_FORGET = 1e-20
N_LRU_BLOCKS = 8
LRU_BLOCK = D_MODEL // N_LRU_BLOCKS
CONV_W = 4
LRU_C = 8.0
HGRN_HEAD_DIM = 128
N_HGRN_HEADS = D_MODEL // HGRN_HEAD_DIM
ATT_HEAD_DIM = 64
N_ATT_HEADS = D_MODEL // ATT_HEAD_DIM
N_HEAD_PAIRS = N_ATT_HEADS // 2
ATT_SCALE = ATT_HEAD_DIM ** -0.5
PAST_CHUNKS = 8
BAND_PAST = PAST_CHUNKS * CHUNK
MAX_REL = 128
MASK_VALUE = -1e30
D_FF = 2816
N_EXPERTS = 8
TOP_K = 2
N_PROJ_BLOCKS = 12

LANES = 128
SUBLANES = 8
TOKEN_TILE_ROWS = D_MODEL // LANES
VMEM_LIMIT_BYTES = 56 * 1024 * 1024


def _cparams(*semantics):
    return pltpu.CompilerParams(dimension_semantics=semantics,
                                vmem_limit_bytes=VMEM_LIMIT_BYTES)


def _split3(x):
    hi = x.astype(BF16)
    r1 = x - hi.astype(F32)
    mid = r1.astype(BF16)
    lo = (r1 - mid.astype(F32)).astype(BF16)
    return hi, mid, lo


def _dot(a, b):
    return jnp.dot(a, b, preferred_element_type=F32)


def _dot_nt(a, b):
    return lax.dot_general(a, b, (((1,), (1,)), ((), ())), preferred_element_type=F32)


def _dot_tn(a, b):
    return lax.dot_general(a, b, (((0,), (0,)), ((), ())), preferred_element_type=F32)


def _sigmoid(x):
    return 1.0 / (1.0 + jnp.exp(-x))


def _sigmoid_tanh(x):
    return 0.5 * jnp.tanh(0.5 * x) + 0.5


def _silu(x):
    half = 0.5 * x
    return half * (1.0 + jnp.tanh(half))


_Q_COL, _K_COL, _V_COL = 6, 7, 8
_F32_COLS = (0, 3)
_BF16_COLS = (1, 2, 4, 5, 9, 10, 11)
AX, BF = 0, 1
AG, BQ, BV, BG, GA, GB, GC = range(7)


def _head_indicator():
    e = np.zeros((D_MODEL, LANES), np.float32)
    e[np.arange(D_MODEL), np.arange(D_MODEL) // ATT_HEAD_DIM] = 1.0
    return e


def _in_proj_kernel(x_ref, g_ref, w_ref, qg_ref, kg_ref, e_ref, et_ref,
                    o32_ref, o16_ref, q16_ref, k16_ref, v16_ref, k32_ref, v32_ref, xn_ref,
                    *, tiles_per_tail, keep):
    i = pl.program_id(0)
    j = pl.program_id(1)
    tm = x_ref.shape[0]

    @pl.when(j == 0)
    def _():
        x = x_ref[...]
        ms = jnp.mean(x * x, axis=-1, keepdims=True)
        xn_ref[...] = (x * lax.rsqrt(ms + EPS) * g_ref[...]).astype(BF16)

    acc = lambda: _dot(xn_ref[...], w_ref[...])
    is_tail = (i % tiles_per_tail) == tiles_per_tail - 1

    def head_norm(x, gain):
        ms = _dot((x * x).astype(BF16), e_ref[...]) * (1.0 / ATT_HEAD_DIM)
        rhi, rmid, _ = _split3(lax.rsqrt(ms + EPS))
        return x * (_dot(rhi, et_ref[...]) + _dot(rmid, et_ref[...])) * gain

    def is_one_of(cols):
        hit = j == cols[0]
        for c in cols[1:]:
            hit = hit | (j == c)
        return hit

    @pl.when(is_one_of(_F32_COLS))
    def _():
        o32_ref[...] = acc()

    @pl.when(is_one_of(_BF16_COLS))
    def _():
        o16_ref[...] = acc().astype(BF16)

    @pl.when(j == _Q_COL)
    def _():
        q16_ref[...] = (head_norm(acc(), qg_ref[...]) * ATT_SCALE).astype(BF16)

    @pl.when(j == _K_COL)
    def _():
        kn = head_norm(acc(), kg_ref[...])
        k16_ref[...] = kn.astype(BF16)

        @pl.when(is_tail)
        def _():
            k32_ref[...] = kn[tm - keep:, :]

    @pl.when(j == _V_COL)
    def _():
        v = acc()
        v16_ref[...] = v.astype(BF16)

        @pl.when(is_tail)
        def _():
            v32_ref[...] = v[tm - keep:, :]


def _in_proj(x, g, w, q_gain, k_gain, seq, keep):
    n, d = x.shape
    tn = d
    if keep == seq:
        tm = min(n, 512)
        tiles_per_tail, keep_rows = 1, tm
    else:
        tm = min(n, 1024)
        assert seq % tm == 0 and keep <= tm
        tiles_per_tail, keep_rows = seq // tm, keep
    n_tail = n // (tiles_per_tail * tm) * keep_rows
    e = _head_indicator()
    tile_gain = lambda gain: jnp.tile(gain, N_ATT_HEADS).reshape(1, d)
    def stored_at(cols):
        def index_map(i, j):
            pos = sum((j >= c).astype(jnp.int32) for c in cols[1:])
            return (pos, i, 0)
        return index_map

    def written_at(col, tiles):
        first = (tiles - 1) * N_PROJ_BLOCKS + col
        return lambda i, j: (jnp.maximum(i * N_PROJ_BLOCKS + j - first, 0) // (tiles * N_PROJ_BLOCKS), 0)

    row16 = lambda col: pl.BlockSpec((tm, d), written_at(col, 1))
    tail = lambda col: pl.BlockSpec((keep_rows, d), written_at(col, tiles_per_tail))
    vec = lambda: pl.BlockSpec((1, d), lambda i, j: (0, 0))
    return pl.pallas_call(
        functools.partial(_in_proj_kernel, tiles_per_tail=tiles_per_tail, keep=keep_rows),
        grid=(n // tm, N_PROJ_BLOCKS),
        in_specs=[
            pl.BlockSpec((tm, d), lambda i, j: (i, 0)),
            vec(),
            pl.BlockSpec((d, tn), lambda i, j: (0, j)),
            vec(), vec(),
            pl.BlockSpec((d, LANES), lambda i, j: (0, 0)),
            pl.BlockSpec((LANES, d), lambda i, j: (0, 0)),
        ],
        out_specs=[pl.BlockSpec((None, tm, tn), stored_at(_F32_COLS)),
                   pl.BlockSpec((None, tm, tn), stored_at(_BF16_COLS)),
                   row16(_Q_COL), row16(_K_COL), row16(_V_COL), tail(_K_COL), tail(_V_COL)],
        out_shape=[jax.ShapeDtypeStruct((len(_F32_COLS), n, d), F32),
                   jax.ShapeDtypeStruct((len(_BF16_COLS), n, d), BF16)]
                  + [jax.ShapeDtypeStruct((n, d), BF16)] * 3
                  + [jax.ShapeDtypeStruct((n_tail, d), F32)] * 2,
        scratch_shapes=[pltpu.VMEM((tm, d), BF16)],
        compiler_params=_cparams("arbitrary", "arbitrary"),
        name="norm_in_proj",
    )(x, g.reshape(1, d), w, tile_gain(q_gain), tile_gain(k_gain),
      jnp.asarray(e, BF16), jnp.asarray(e.T, BF16))


_XP_PAD = SUBLANES


def _lru_kernel(ax_ref, ag_ref, ga_ref, conv0_ref, h0_ref, cw_ref, cb_ref, wri_ref, br_ref, bi_ref,
                lam_ref, y_ref, conv_out_ref, h_out_ref, xp_ref, a_ref, u_ref, h_ref, hc_ref):
    i = pl.program_id(1)
    tt = ax_ref.shape[0]
    keep = CONV_W - 1

    @pl.when(i == 0)
    def _():
        xp_ref[_XP_PAD - keep:_XP_PAD, :] = conv0_ref[0]
        hc_ref[...] = h0_ref[0]

    x = ax_ref[...]
    xp_ref[_XP_PAD:_XP_PAD + tt, :] = x
    conv = cb_ref[...] + cw_ref[CONV_W - 1:CONV_W, :] * x
    for j in range(CONV_W - 1):
        off = _XP_PAD - (CONV_W - 1 - j)
        conv = conv + cw_ref[j:j + 1, :] * xp_ref[off:off + tt, :]
    tail = xp_ref[_XP_PAD + tt - keep:_XP_PAD + tt, :]
    xp_ref[_XP_PAD - keep:_XP_PAD, :] = tail
    conv_out_ref[0] = tail

    cb16 = conv.astype(BF16)
    r_parts, i_parts = [], []
    for n in range(N_LRU_BLOCKS):
        g = _dot(cb16[:, n * LRU_BLOCK:(n + 1) * LRU_BLOCK], wri_ref[n])
        r_parts.append(g[:, :LRU_BLOCK])
        i_parts.append(g[:, LRU_BLOCK:])
    r = _sigmoid_tanh(jnp.concatenate(r_parts, axis=-1) + br_ref[...])
    ig = _sigmoid_tanh(jnp.concatenate(i_parts, axis=-1) + bi_ref[...])
    lam = lam_ref[...]
    softplus_neg = jnp.maximum(-lam, 0.0) + jnp.log1p(jnp.exp(-jnp.abs(lam)))
    log_a = -LRU_C * r * softplus_neg
    a = jnp.exp(log_a)
    th = jnp.tanh(log_a)
    u = jnp.sqrt(jnp.maximum(-2.0 * th / (1.0 - th), 0.0)) * ig * conv

    row = lax.broadcasted_iota(jnp.int32, a.shape, 0) % SUBLANES
    d = 1
    while d < SUBLANES:
        ok = row >= d
        a_sh = jnp.where(ok, pltpu.roll(a, d, 0), 1.0)
        u_sh = jnp.where(ok, pltpu.roll(u, d, 0), 0.0)
        u = a * u_sh + u
        a = a * a_sh
        d *= 2
    a_ref[...] = a
    u_ref[...] = u

    def group(gi, h):
        r0 = pl.multiple_of(gi * SUBLANES, SUBLANES)
        hh = a_ref[pl.ds(r0, SUBLANES), :] * h + u_ref[pl.ds(r0, SUBLANES), :]
        h_ref[pl.ds(r0, SUBLANES), :] = hh
        return hh[SUBLANES - 1:SUBLANES, :]

    h_last = lax.fori_loop(0, tt // SUBLANES, group, hc_ref[...], unroll=4)
    hc_ref[...] = h_last
    h_out_ref[0] = h_last
    y_ref[...] = (_sigmoid_tanh(ga_ref[...].astype(F32)) * h_ref[...]
                  * jax.nn.gelu(ag_ref[...].astype(F32))).astype(BF16)


def _lru_branch(proj32, proj16, conv0, h0, conv_w, conv_b, wri, b_r, b_i, lam, batch, seq):
    d = D_MODEL
    tt = min(seq, 256)
    nt = seq // tt
    row = lambda b, i: b * nt + i
    vec = lambda: pl.BlockSpec((1, d), lambda b, i: (0, 0))
    return pl.pallas_call(
        _lru_kernel,
        grid=(batch, nt),
        in_specs=[
            pl.BlockSpec((None, tt, d), lambda b, i: (AX, row(b, i), 0)),
            pl.BlockSpec((None, tt, d), lambda b, i: (AG, row(b, i), 0)),
            pl.BlockSpec((None, tt, d), lambda b, i: (GA, row(b, i), 0)),
            pl.BlockSpec((1, CONV_W - 1, d), lambda b, i: (b, 0, 0)),
            pl.BlockSpec((1, 1, d), lambda b, i: (b, 0, 0)),
            pl.BlockSpec((CONV_W, d), lambda b, i: (0, 0)),
            vec(),
            pl.BlockSpec((N_LRU_BLOCKS, LRU_BLOCK, 2 * LRU_BLOCK), lambda b, i: (0, 0, 0)),
            vec(), vec(), vec(),
        ],
        out_specs=[
            pl.BlockSpec((tt, d), lambda b, i: (row(b, i), 0)),
            pl.BlockSpec((1, CONV_W - 1, d), lambda b, i: (b, 0, 0)),
            pl.BlockSpec((1, 1, d), lambda b, i: (b, 0, 0)),
        ],
        out_shape=[
            jax.ShapeDtypeStruct((batch * seq, d), BF16),
            jax.ShapeDtypeStruct((batch, CONV_W - 1, d), F32),
            jax.ShapeDtypeStruct((batch, 1, d), F32),
        ],
        scratch_shapes=[
            pltpu.VMEM((_XP_PAD + tt, d), F32),
            pltpu.VMEM((tt, d), F32),
            pltpu.VMEM((tt, d), F32),
            pltpu.VMEM((tt, d), F32),
            pltpu.VMEM((1, d), F32),
        ],
        compiler_params=_cparams("parallel", "arbitrary"),
        name="lru_branch",
    )(proj32, proj16, proj16, conv0, h0.reshape(batch, 1, d), conv_w, conv_b.reshape(1, d), wri,
      b_r.reshape(1, d), b_i.reshape(1, d), lam.reshape(1, d))


_HGRN_BLOCK = 128
_HGRN_BLOCKS_PER_STEP = 4


def _hgrn_levels(block):
    return tuple(block >> (i + 1) for i in range(block.bit_length() - 1))


def _hgrn_masks(block):
    t = np.arange(block)[:, None]
    s = np.arange(block)[None, :]
    masks = []
    for g in _hgrn_levels(block):
        masks.append((t // (2 * g) == s // (2 * g)) & (t % (2 * g) >= g) & (s % (2 * g) < g))
    masks.append(t == s)
    return np.stack(masks).astype(np.float32)


def _hgrn_kernel(q_ref, f_ref, v_ref, g_ref, gb_ref, s0_ref, lb_ref, gn_ref, tri_ref, mask_ref,
                 y_ref, s_out_ref, st_ref, b_ref, *, block, blocks_per_step, n_steps):
    step = pl.program_id(1)

    def load_state():
        for h in range(N_HGRN_HEADS):
            st_ref[h] = s0_ref[0, h].T

    def store_state():
        for h in range(N_HGRN_HEADS):
            s_out_ref[0, h] = st_ref[h].T

    if n_steps == 1:
        load_state()
    else:
        pl.when(step == 0)(load_state)

    lb = lb_ref[...]
    gn = gn_ref[...]
    tri = tri_ref[...]
    rowi = lax.broadcasted_iota(jnp.int32, (block, D_MODEL), 0)

    for ci in range(blocks_per_step):
        rows = slice(ci * block, (ci + 1) * block)
        th = jnp.tanh(0.5 * f_ref[rows, :])
        f = lb + (1.0 - lb) * (0.5 + 0.5 * th)
        kb = (1.0 - lb) * (0.5 - 0.5 * th)
        qs = _silu(q_ref[rows, :].astype(F32))
        v16 = v_ref[rows, :]

        hi, mid, lo = _split3(jnp.log2(jnp.maximum(f, MIN_FORGET)))
        b = _dot(tri, hi) + _dot(tri, mid) + _dot(tri, lo)
        bc_ref = b_ref.at[ci]
        bc_ref[...] = b
        b_last = bc_ref[block - 1:block, :]

        q_inter = (qs * jnp.exp2(b)).astype(BF16)
        k_state = (kb * jnp.exp2(b_last - b)).astype(BF16)

        q_lv, k_lv = [], []
        for g in _hgrn_levels(block):
            if g >= 4:
                blocks = [jnp.broadcast_to(bc_ref[m * 2 * g + g - 1:m * 2 * g + g, :], (2 * g, D_MODEL))
                          for m in range(block // (2 * g))]
                ref = jnp.concatenate(blocks, axis=0) if len(blocks) > 1 else blocks[0]
            elif g == 2:
                p = rowi % 4
                ref = jnp.where(p == 0, pltpu.roll(b, block - 1, 0),
                                jnp.where(p == 1, b, jnp.where(p == 2, pltpu.roll(b, 1, 0),
                                                               pltpu.roll(b, 2, 0))))
            else:
                ref = jnp.where(rowi % 2 == 0, b, pltpu.roll(b, 1, 0))
            fac = jnp.exp2(-jnp.abs(b - ref))
            q_lv.append((qs * fac).astype(BF16))
            k_lv.append((kb * fac).astype(BF16))
        q_lv.append(qs.astype(BF16))
        k_lv.append(kb.astype(BF16))

        gate = _silu(g_ref[rows, :].astype(F32)) * _sigmoid_tanh(gb_ref[rows, :].astype(F32))
        for h in range(N_HGRN_HEADS):
            sl = slice(h * HGRN_HEAD_DIM, (h + 1) * HGRN_HEAD_DIM)
            st = st_ref[h]
            att = mask_ref[0] * _dot_nt(q_lv[0][:, sl], k_lv[0][:, sl])
            for l in range(1, len(q_lv)):
                att = att + mask_ref[l] * _dot_nt(q_lv[l][:, sl], k_lv[l][:, sl])
            o = _dot_nt(q_inter[:, sl], st.astype(BF16)) + _dot(att.astype(BF16), v16[:, sl])
            ms = jnp.mean(o * o, axis=-1, keepdims=True)
            y_ref[rows, sl] = (o * lax.rsqrt(ms + EPS) * gn[:, sl] * gate[:, sl]).astype(BF16)
            st_ref[h] = st * jnp.exp2(b_last[:, sl]) + _dot_tn(v16[:, sl], k_state[:, sl])

    if n_steps == 1:
        store_state()
    else:
        pl.when(step == n_steps - 1)(store_state)


def _hgrn_branch(proj32, proj16, s0, lb, gn, batch, seq):
    d = D_MODEL
    block = min(seq, _HGRN_BLOCK)
    bps = min(seq // block, _HGRN_BLOCKS_PER_STEP)
    tt = bps * block
    nt = seq // tt
    row = lambda b, c: b * nt + c
    tri = jnp.asarray(np.tril(np.ones((block, block), np.float32)), BF16)
    masks = jnp.asarray(_hgrn_masks(block))
    col = lambda j: pl.BlockSpec((None, tt, d), lambda b, c: (j, row(b, c), 0))
    state = lambda: pl.BlockSpec((1, N_HGRN_HEADS, HGRN_HEAD_DIM, HGRN_HEAD_DIM),
                                 lambda b, c: (b, 0, 0, 0))
    vec = lambda: pl.BlockSpec((1, d), lambda b, c: (0, 0))
    return pl.pallas_call(
        functools.partial(_hgrn_kernel, block=block, blocks_per_step=bps, n_steps=nt),
        grid=(batch, nt),
        in_specs=[col(BQ), col(BF), col(BV), col(BG), col(GB), state(), vec(), vec(),
                  pl.BlockSpec((block, block), lambda b, c: (0, 0)),
                  pl.BlockSpec(masks.shape, lambda b, c: (0, 0, 0))],
        out_specs=[pl.BlockSpec((tt, d), lambda b, c: (row(b, c), 0)), state()],
        out_shape=[jax.ShapeDtypeStruct((batch * seq, d), BF16),
                   jax.ShapeDtypeStruct(s0.shape, F32)],
        scratch_shapes=[pltpu.VMEM((N_HGRN_HEADS, HGRN_HEAD_DIM, HGRN_HEAD_DIM), F32),
                        pltpu.VMEM((bps, block, d), F32)],
        compiler_params=_cparams("parallel", "arbitrary"),
        name="hgrn_branch",
    )(proj16, proj32, proj16, proj16, proj16, s0, lb.reshape(1, d), gn.reshape(1, d), tri, masks)


_ATT_PAIRS_PER_STEP = 8
_ATT_LANES = _ATT_PAIRS_PER_STEP * LANES
_ATT_STEPS = N_HEAD_PAIRS // _ATT_PAIRS_PER_STEP


def _attn_kernel(*refs, n_kv, dynamic_valid, tq):
    q_ref = refs[0]
    k_refs = refs[1:1 + n_kv]
    v_refs = refs[1 + n_kv:1 + 2 * n_kv]
    bias_ref = refs[1 + 2 * n_kv]
    gc_ref = refs[2 + 2 * n_kv]
    o_ref = refs[3 + 2 * n_kv]

    lane = lax.broadcasted_iota(jnp.int32, (tq, LANES), 1)
    lo_half = lane < ATT_HEAD_DIM
    for pi in range(q_ref.shape[1] // LANES):
        ls = slice(pi * LANES, (pi + 1) * LANES)
        q = q_ref[:, ls]
        k = jnp.concatenate([r[:, ls] for r in k_refs], axis=0) if n_kv > 1 else k_refs[0][:, ls]
        v = jnp.concatenate([r[:, ls] for r in v_refs], axis=0) if n_kv > 1 else v_refs[0][:, ls]
        zero = jnp.zeros_like(q)
        outs = []
        for hh, qm in enumerate((jnp.where(lo_half, q, zero), jnp.where(lo_half, zero, q))):
            s = _dot_nt(qm, k) + bias_ref[2 * pi + hh]
            if dynamic_valid:
                first_valid = BAND_PAST - pl.program_id(2) * tq
                colj = lax.broadcasted_iota(jnp.int32, s.shape, 1)
                s = jnp.where(colj >= first_valid, s, MASK_VALUE)
            m = jnp.max(s, axis=-1, keepdims=True)
            p = jnp.exp(s - m)
            denom = jnp.sum(p, axis=-1, keepdims=True)
            outs.append(_dot(p.astype(BF16), v) / denom)
        o_ref[:, ls] = (jnp.where(lo_half, outs[0], outs[1])
                        * _sigmoid_tanh(gc_ref[:, ls].astype(F32))).astype(BF16)


def _band_bias(table, tq, n_keys, banded):
    t = np.arange(tq)[:, None]
    j = np.arange(n_keys)[None, :]
    width = tq + n_keys - 1
    dist = BAND_PAST + (tq - 1) - np.arange(width + 1)
    diag = table[:, np.clip(dist, -MAX_REL, MAX_REL) + MAX_REL]
    skew = jnp.tile(diag, (1, tq))[:, :tq * width].reshape(-1, tq, width)
    bias = skew[:, :, tq - 1:tq - 1 + n_keys]
    if banded:
        c0 = (t // CHUNK) * CHUNK
        in_band = (j >= c0) & (j < c0 + BAND_PAST + CHUNK)
        bias = jnp.where(jnp.asarray(in_band)[None], bias, MASK_VALUE)
    return bias


def _attention_prompt(q16, k16, v16, proj16, table, batch, seq):
    tq = 256
    nt = seq // tq
    n_kv = BAND_PAST // tq + 1
    bias = _band_bias(table, tq, n_kv * tq, True)
    kv = lambda back: pl.BlockSpec(
        (tq, _ATT_LANES), lambda p, b, i: (b * nt + jnp.maximum(i - back, 0), p))
    kv_specs = [kv(n_kv - 1 - j) for j in range(n_kv)]
    return pl.pallas_call(
        functools.partial(_attn_kernel, n_kv=n_kv, dynamic_valid=True, tq=tq),
        grid=(_ATT_STEPS, batch, nt),
        in_specs=[pl.BlockSpec((tq, _ATT_LANES), lambda p, b, i: (b * nt + i, p))] + kv_specs + kv_specs
                 + [pl.BlockSpec((2 * _ATT_PAIRS_PER_STEP, tq, n_kv * tq), lambda p, b, i: (p, 0, 0)),
                    pl.BlockSpec((None, tq, _ATT_LANES), lambda p, b, i: (GC, b * nt + i, p))],
        out_specs=pl.BlockSpec((tq, _ATT_LANES), lambda p, b, i: (b * nt + i, p)),
        out_shape=jax.ShapeDtypeStruct((batch * seq, D_MODEL), BF16),
        compiler_params=_cparams("parallel", "parallel", "arbitrary"),
        name="band_attention_prompt",
    )(q16, *([k16] * n_kv), *([v16] * n_kv), bias, proj16)


def _attention_sample(q16, k16, v16, proj16, cache_k16, cache_v16, table, batch, seq):
    cl = cache_k16.shape[0] // batch
    bias = _band_bias(table, seq, cl + seq, False)
    own = pl.BlockSpec((seq, _ATT_LANES), lambda p, b: (b, p))
    cache = pl.BlockSpec((cl, _ATT_LANES), lambda p, b: (b, p))
    return pl.pallas_call(
        functools.partial(_attn_kernel, n_kv=2, dynamic_valid=False, tq=seq),
        grid=(_ATT_STEPS, batch),
        in_specs=[own, cache, own, cache, own,
                  pl.BlockSpec((2 * _ATT_PAIRS_PER_STEP, seq, cl + seq), lambda p, b: (p, 0, 0)),
                  pl.BlockSpec((None, seq, _ATT_LANES), lambda p, b: (GC, b, p))],
        out_specs=own,
        out_shape=jax.ShapeDtypeStruct((batch * seq, D_MODEL), BF16),
        compiler_params=_cparams("parallel", "parallel"),
        name="band_attention_sample",
    )(q16, cache_k16, k16, cache_v16, v16, bias, proj16)


def _store_token_tiles(ref, x):
    rows = x.shape[0]
    for s in range(TOKEN_TILE_ROWS):
        ref[pl.ds(s, rows, stride=TOKEN_TILE_ROWS), :] = x[:, s * LANES:(s + 1) * LANES]


def _load_token_tiles(ref, rows, s):
    return ref[pl.ds(s, rows, stride=TOKEN_TILE_ROWS), :]


def _merge_kernel(ya_ref, yb_ref, yc_ref, x_ref, w_ref, gn_ref, *rest,
                  with_router):
    if with_router:
        wr_ref, x1_ref, xn_ref, route_ref = rest
    else:
        x1_ref, xn_ref = rest
    merged = ya_ref[...].astype(F32) + yb_ref[...].astype(F32) + yc_ref[...].astype(F32)
    x1 = x_ref[...] + _dot(merged.astype(BF16), w_ref[...])
    x1_ref[...] = x1
    ms = jnp.mean(x1 * x1, axis=-1, keepdims=True)
    xn = x1 * lax.rsqrt(ms + EPS) * gn_ref[...]
    if not with_router:
        xn_ref[...] = xn.astype(BF16)
    else:
        _store_token_tiles(xn_ref, xn)
        xh, xl, _ = _split3(xn)
        wh = wr_ref[0]
        wl = wr_ref[1]
        logits = _dot(xh, wh) + (_dot(xh, wl) + _dot(xl, wh))
        lane = lax.broadcasted_iota(jnp.int32, logits.shape, 1).astype(F32)
        neg = jnp.float32(-jnp.inf)
        logits = jnp.where(lane < N_EXPERTS, logits, neg)
        m1 = jnp.max(logits, axis=-1, keepdims=True)
        i1 = jnp.min(jnp.where(logits == m1, lane, float(LANES)), axis=-1, keepdims=True)
        rest_l = jnp.where(lane == i1, neg, logits)
        m2 = jnp.max(rest_l, axis=-1, keepdims=True)
        i2 = jnp.min(jnp.where(rest_l == m2, lane, float(LANES)), axis=-1, keepdims=True)
        e2 = jnp.exp(m2 - m1)
        p1 = 1.0 / (1.0 + e2)
        p2 = e2 / (1.0 + e2)
        route_ref[...] = jnp.where(lane == 0.0, i1, jnp.where(lane == 1.0, i2,
                                   jnp.where(lane == 2.0, p1, jnp.where(lane == 3.0, p2, 0.0))))


def _merge_out(ya, yb, yc, x, w_out16, ffn_gain, router=None):
    n, d = x.shape
    tm = min(n, 256)
    rowspec = pl.BlockSpec((tm, d), lambda i: (i, 0))
    in_specs = [rowspec, rowspec, rowspec, rowspec,
                pl.BlockSpec((d, d), lambda i: (0, 0)),
                pl.BlockSpec((1, d), lambda i: (0, 0))]
    args = [ya, yb, yc, x, w_out16, ffn_gain.reshape(1, d)]
    out_specs = [rowspec, rowspec]
    out_shape = [jax.ShapeDtypeStruct((n, d), F32), jax.ShapeDtypeStruct((n, d), BF16)]
    if router is not None:
        out_specs[1] = pl.BlockSpec((tm * TOKEN_TILE_ROWS, LANES), lambda i: (i, 0))
        out_shape[1] = jax.ShapeDtypeStruct((n * TOKEN_TILE_ROWS, LANES), F32)
        wr = jnp.pad(router, ((0, 0), (0, LANES - N_EXPERTS)))
        wh = wr.astype(BF16)
        wl = (wr - wh.astype(F32)).astype(BF16)
        in_specs.append(pl.BlockSpec((2, d, LANES), lambda i: (0, 0, 0)))
        args.append(jnp.stack([wh, wl]))
        out_specs.append(pl.BlockSpec((tm, LANES), lambda i: (i, 0)))
        out_shape.append(jax.ShapeDtypeStruct((n, LANES), F32))
    return pl.pallas_call(
        functools.partial(_merge_kernel, with_router=router is not None),
        grid=(n // tm,),
        in_specs=in_specs,
        out_specs=out_specs,
        out_shape=out_shape,
        compiler_params=_cparams("parallel"),
        name="merge_out_proj",
    )(*args)


_FF_BLOCK = D_FF // 2
_N_FF_BLOCKS = D_FF // _FF_BLOCK


def _swiglu_block(xn16, wg, wu, wd):
    hg = _dot(xn16, wg)
    hu = _dot(xn16, wu)
    return _dot((_silu(hg) * hu).astype(BF16), wd)


def _ffn_kernel(xn_ref, x1_ref, wg_ref, wu_ref, wd_ref, o_ref, acc_ref):
    j = pl.program_id(1)
    y = _swiglu_block(xn_ref[...], wg_ref[...], wu_ref[...], wd_ref[...])

    @pl.when(j == 0)
    def _():
        acc_ref[...] = x1_ref[...] + y

    @pl.when(j > 0)
    def _():
        acc_ref[...] += y

    @pl.when(j == _N_FF_BLOCKS - 1)
    def _():
        o_ref[...] = acc_ref[...]


def _ffn(xn16, x1, wg16, wu16, wd16):
    n, d = x1.shape
    tm = min(n, 512)
    rowspec = lambda: pl.BlockSpec((tm, d), lambda i, j: (i, 0))
    return pl.pallas_call(
        _ffn_kernel,
        grid=(n // tm, _N_FF_BLOCKS),
        in_specs=[rowspec(), rowspec(),
                  pl.BlockSpec((d, _FF_BLOCK), lambda i, j: (0, j)),
                  pl.BlockSpec((d, _FF_BLOCK), lambda i, j: (0, j)),
                  pl.BlockSpec((_FF_BLOCK, d), lambda i, j: (j, 0))],
        out_specs=rowspec(),
        out_shape=jax.ShapeDtypeStruct((n, d), F32),
        scratch_shapes=[pltpu.VMEM((tm, d), F32)],
        compiler_params=_cparams("parallel", "arbitrary"),
        name="swiglu_ffn",
    )(xn16, x1, wg16, wu16, wd16)


_DISPATCH_CHUNK = 2048
_COMBINE_TOKENS = 256


def _routing_tables(route, n, tm):
    experts = route[:, :TOP_K].astype(jnp.int32).reshape(-1)
    onehot = (jnp.arange(N_EXPERTS, dtype=jnp.int32)[:, None] == experts[None]).astype(jnp.int32)
    running = jnp.cumsum(onehot, axis=1)
    rank = jnp.sum(running * onehot, axis=0) - 1
    count = running[:, -1]
    group = (count + tm - 1) // tm * tm
    group_end = jnp.cumsum(group)
    group_start = group_end - group
    dest = jnp.sum(onehot * group_start[:, None], axis=0) + rank
    n_tiles = TOP_K * n // tm + N_EXPERTS
    tile_start = jnp.arange(n_tiles, dtype=jnp.int32) * tm
    tile_expert = jnp.minimum(jnp.sum(tile_start[:, None] >= group_end[None], axis=1), N_EXPERTS - 1)
    n_used = (group_end[-1] // tm).reshape(1)
    return dest.astype(jnp.int32), tile_expert.astype(jnp.int32), n_used.astype(jnp.int32)


def _token_rows(index):
    return pl.ds(pl.multiple_of(index * TOKEN_TILE_ROWS, TOKEN_TILE_ROWS), TOKEN_TILE_ROWS)


def _dispatch_kernel(dest_ref, src_ref, init_ref, dst_ref, sem):
    del init_ref
    chunk = dest_ref.shape[2]

    def issue(token, carry):
        for k in range(TOP_K):
            pltpu.make_async_copy(src_ref.at[_token_rows(token)],
                                  dst_ref.at[_token_rows(dest_ref[0, 0, TOP_K * token + k])],
                                  sem).start()
        return carry

    lax.fori_loop(0, chunk // TOP_K, issue, 0)
    for k in range(TOP_K):
        pltpu.make_async_copy(src_ref, dst_ref.at[pl.ds(0, src_ref.shape[0])], sem).wait()


def _dispatch(xn_tiles, dest, n_slots):
    n_assign = dest.shape[0]
    chunk = min(n_assign, _DISPATCH_CHUNK)
    steps = n_assign // chunk
    shape = jax.ShapeDtypeStruct((n_slots * TOKEN_TILE_ROWS, LANES), F32)
    return pl.pallas_call(
        _dispatch_kernel,
        grid=(steps,),
        in_specs=[pl.BlockSpec((1, 1, chunk), lambda i: (i, 0, 0), memory_space=pltpu.SMEM),
                  pl.BlockSpec((chunk // TOP_K * TOKEN_TILE_ROWS, LANES), lambda i: (i, 0)),
                  pl.BlockSpec(memory_space=pl.ANY)],
        out_specs=pl.BlockSpec(memory_space=pl.ANY),
        out_shape=shape,
        scratch_shapes=[pltpu.SemaphoreType.DMA(())],
        input_output_aliases={2: 0},
        compiler_params=_cparams("arbitrary"),
        name="moe_dispatch",
    )(dest.reshape(steps, 1, chunk), xn_tiles, jnp.zeros(shape.shape, F32))


def _expert_ffn_kernel(te_ref, nu_ref, x_ref, wg_ref, wu_ref, wd_ref, y_ref, xs_ref, acc_ref):
    del te_ref
    i = pl.program_id(0)
    j = pl.program_id(1)
    tm = xs_ref.shape[0]
    used = i < nu_ref[0]

    @pl.when(used & (j == 0))
    def _():
        for s in range(TOKEN_TILE_ROWS):
            xs_ref[:, s * LANES:(s + 1) * LANES] = _load_token_tiles(x_ref, tm, s).astype(BF16)

    @pl.when(used)
    def _():
        y = _swiglu_block(xs_ref[...], wg_ref[0], wu_ref[0], wd_ref[0])

        @pl.when(j == 0)
        def _():
            acc_ref[...] = y

        @pl.when(j > 0)
        def _():
            acc_ref[...] += y

    @pl.when(used & (j == _N_FF_BLOCKS - 1))
    def _():
        _store_token_tiles(y_ref, acc_ref[...])

    @pl.when(jnp.logical_not(used) & (j == _N_FF_BLOCKS - 1))
    def _():
        y_ref[...] = jnp.zeros(y_ref.shape, F32)


def _expert_ffn(x_sorted, tile_expert, n_used, wg16, wu16, wd16, tm):
    d = D_MODEL
    n_tiles = tile_expert.shape[0]
    rows = pl.BlockSpec((tm * TOKEN_TILE_ROWS, LANES), lambda i, j, te, nu: (i, 0))
    grid_spec = pltpu.PrefetchScalarGridSpec(
        num_scalar_prefetch=2,
        grid=(n_tiles, _N_FF_BLOCKS),
        in_specs=[rows,
                  pl.BlockSpec((1, d, _FF_BLOCK), lambda i, j, te, nu: (te[i], 0, j)),
                  pl.BlockSpec((1, d, _FF_BLOCK), lambda i, j, te, nu: (te[i], 0, j)),
                  pl.BlockSpec((1, _FF_BLOCK, d), lambda i, j, te, nu: (te[i], j, 0))],
        out_specs=rows,
        scratch_shapes=[pltpu.VMEM((tm, d), BF16), pltpu.VMEM((tm, d), F32)])
    return pl.pallas_call(
        _expert_ffn_kernel,
        grid_spec=grid_spec,
        out_shape=jax.ShapeDtypeStruct(x_sorted.shape, F32),
        compiler_params=_cparams("arbitrary", "arbitrary"),
        name="moe_expert_ffn",
    )(tile_expert, n_used, x_sorted, wg16, wu16, wd16)


def _combine_kernel(dest_ref, next_dest_ref, x1_ref, route_ref, y_ref, o_ref, buf_ref, sem):
    i = pl.program_id(0)
    last = pl.num_programs(0) - 1
    tc = x1_ref.shape[0]

    def copies(idx_ref, slot, t, k):
        return pltpu.make_async_copy(y_ref.at[_token_rows(idx_ref[0, 0, TOP_K * t + k])],
                                     buf_ref.at[slot, k, _token_rows(t)], sem.at[slot])

    def issue(idx_ref, slot):
        def body(t, carry):
            for k in range(TOP_K):
                copies(idx_ref, slot, t, k).start()
            return carry
        lax.fori_loop(0, tc, body, 0)

    def finish(slot):
        for k in range(TOP_K):
            pltpu.make_async_copy(y_ref.at[pl.ds(0, tc * TOKEN_TILE_ROWS)], buf_ref.at[slot, k],
                                  sem.at[slot]).wait()
        route = route_ref[...]
        p1 = route[:, TOP_K:TOP_K + 1]
        p2 = route[:, TOP_K + 1:TOP_K + 2]
        for s in range(TOKEN_TILE_ROWS):
            sl = slice(s * LANES, (s + 1) * LANES)
            o_ref[:, sl] = (x1_ref[:, sl] + p1 * _load_token_tiles(buf_ref.at[slot, 0], tc, s)
                            + p2 * _load_token_tiles(buf_ref.at[slot, 1], tc, s))

    @pl.when(i == 0)
    def _():
        issue(dest_ref, 0)

    for slot in range(2):
        @pl.when(i % 2 == slot)
        def _(slot=slot):
            @pl.when(i < last)
            def _():
                issue(next_dest_ref, 1 - slot)
            finish(slot)


def _combine(x1, route, y_sorted, dest):
    n, d = x1.shape
    tc = min(n, _COMBINE_TOKENS)
    steps = n // tc
    dest3 = dest.reshape(steps, 1, TOP_K * tc)
    smem = lambda index_map: pl.BlockSpec((1, 1, TOP_K * tc), index_map, memory_space=pltpu.SMEM)
    return pl.pallas_call(
        _combine_kernel,
        grid=(steps,),
        in_specs=[smem(lambda i: (i, 0, 0)),
                  smem(lambda i: (jnp.minimum(i + 1, steps - 1), 0, 0)),
                  pl.BlockSpec((tc, d), lambda i: (i, 0)),
                  pl.BlockSpec((tc, LANES), lambda i: (i, 0)),
                  pl.BlockSpec(memory_space=pl.ANY)],
        out_specs=pl.BlockSpec((tc, d), lambda i: (i, 0)),
        out_shape=jax.ShapeDtypeStruct((n, d), F32),
        scratch_shapes=[pltpu.VMEM((2, TOP_K, tc * TOKEN_TILE_ROWS, LANES), F32),
                        pltpu.SemaphoreType.DMA((2,))],
        compiler_params=_cparams("arbitrary"),
        name="moe_combine",
    )(dest3, dest3, x1, route, y_sorted)


def _moe(xn_tiles, x1, route, wg16, wu16, wd16):
    n = x1.shape[0]
    tm = 512 if n >= 8192 else 256
    dest, tile_expert, n_used = _routing_tables(route, n, tm)
    x_sorted = _dispatch(xn_tiles, dest, tile_expert.shape[0] * tm)
    y_sorted = _expert_ffn(x_sorted, tile_expert, n_used, wg16, wu16, wd16, tm)
    return _combine(x1, route, y_sorted, dest)


def _mixer(x, layer_w, conv0, h0, s0, cache, batch, seq):
    keep = seq if cache is not None else min(BAND_PAST, seq)
    proj32, proj16, q16, k16, v16, k_tail, v_tail = _in_proj(
        x, layer_w["norm_mix"], layer_w["w_in"], layer_w["q_norm"], layer_w["k_norm"], seq, keep)
    ya, conv_new, h_new = _lru_branch(proj32, proj16, conv0, h0, layer_w["conv_w"], layer_w["conv_b"],
                                      layer_w["wri"], layer_w["lru_br"], layer_w["lru_bi"],
                                      layer_w["lru_lambda"], batch, seq)
    yb, s_new = _hgrn_branch(proj32, proj16, s0, layer_w["lb"], layer_w["hgrn_norm"], batch, seq)
    if cache is None:
        yc = _attention_prompt(q16, k16, v16, proj16, layer_w["rel_table"], batch, seq)
    else:
        yc = _attention_sample(q16, k16, v16, proj16, cache[0], cache[1], layer_w["rel_table"], batch,
                               seq)
    heads = (batch, keep, N_ATT_HEADS, ATT_HEAD_DIM)
    states = (conv_new, h_new.reshape(batch, D_MODEL), s_new, k_tail.reshape(heads), v_tail.reshape(heads))
    return ya, yb, yc, states


def kernel(x_prompt, x_sample, state_conv, state_lru, state_hgrn, cache_k, cache_v, norm_mix, w_in, conv_w, conv_b, lru_wr, lru_br, lru_wi, lru_bi, lru_lambda, hgrn_gamma, hgrn_norm, q_norm, k_norm, rel_bias_table, w_out, norm_ffn, ffn_w_gate, ffn_w_up, ffn_w_down, moe_router, moe_w_gate, moe_w_up, moe_w_down):
    depth = w_in.shape[0]
    bp, sp, d = x_prompt.shape
    bs, ss, _ = x_sample.shape

    p = jax.nn.softmax(hgrn_gamma.astype(F32), axis=0)
    lb_all = jnp.cumsum(p, axis=0) - p

    xp = x_prompt.reshape(bp * sp, d)
    xs = x_sample.reshape(bs * ss, d)
    zero_conv = jnp.zeros((bp, CONV_W - 1, d), F32)
    zero_h = jnp.zeros((bp, d), F32)
    zero_s = jnp.zeros((bp,) + state_hgrn.shape[2:], F32)

    st_p, st_s = [], []
    for l in range(depth):
        layer_w = dict(
            norm_mix=norm_mix[l], w_in=w_in[l].astype(BF16), conv_w=conv_w[l], conv_b=conv_b[l],
            wri=jnp.concatenate([lru_wr[l], lru_wi[l]], axis=-1).astype(BF16),
            lru_br=lru_br[l], lru_bi=lru_bi[l], lru_lambda=lru_lambda[l], lb=lb_all[l],
            hgrn_norm=hgrn_norm[l], q_norm=q_norm[l], k_norm=k_norm[l],
            rel_table=rel_bias_table[l])
        w_out16 = w_out[l].astype(BF16)
        cl = cache_k.shape[2]
        cache = (cache_k[l].reshape(bs * cl, d).astype(BF16),
                 cache_v[l].reshape(bs * cl, d).astype(BF16))
        dense = l % 2 == 0
        if dense:
            wg16 = ffn_w_gate[l // 2].astype(BF16)
            wu16 = ffn_w_up[l // 2].astype(BF16)
            wd16 = ffn_w_down[l // 2].astype(BF16)
            router = None
        else:
            wg16 = moe_w_gate[l // 2].astype(BF16)
            wu16 = moe_w_up[l // 2].astype(BF16)
            wd16 = moe_w_down[l // 2].astype(BF16)
            router = moe_router[l // 2]

        new_x = []
        for x, conv0, h0, s0, cch, batch, seq, acc in (
                (xp, zero_conv, zero_h, zero_s, None, bp, sp, st_p),
                (xs, state_conv[l], state_lru[l], state_hgrn[l], cache, bs, ss, st_s)):
            ya, yb, yc, states = _mixer(x, layer_w, conv0, h0, s0, cch, batch, seq)
            outs = _merge_out(ya, yb, yc, x, w_out16, norm_ffn[l], router)
            if router is None:
                new_x.append(_ffn(outs[1], outs[0], wg16, wu16, wd16))
            else:
                new_x.append(_moe(outs[1], outs[0], outs[2], wg16, wu16, wd16))
            acc.append(states)
        xp, xs = new_x

    stack = lambda sts, i: jnp.stack([s[i] for s in sts])
    return (xp.reshape(bp, sp, d), xs.reshape(bs, ss, d),
            stack(st_p, 0), stack(st_p, 1), stack(st_p, 2), stack(st_p, 3), stack(st_p, 4),
            stack(st_s, 0), stack(st_s, 1), stack(st_s, 2), stack(st_s, 3), stack(st_s, 4))
```

```python
import functools

import numpy as np
import jax
import jax.numpy as jnp
from jax import lax
from jax.experimental import pallas as pl
from jax.experimental.pallas import tpu as pltpu

F32 = jnp.float32
BF16 = jnp.bfloat16

D_MODEL = 1024
CHUNK = 64
EPS = 1e-6
MIN_FORGET = 1e-20
N_LRU_BLOCKS = 8
LRU_BLOCK = D_MODEL // N_LRU_BLOCKS
CONV_W = 4
LRU_C = 8.0
HGRN_HEAD_DIM = 128
N_HGRN_HEADS = D_MODEL // HGRN_HEAD_DIM
ATT_HEAD_DIM = 64
N_ATT_HEADS = D_MODEL // ATT_HEAD_DIM
N_HEAD_PAIRS = N_ATT_HEADS // 2
ATT_SCALE = ATT_HEAD_DIM ** -0.5
PAST_CHUNKS = 8
BAND_PAST = PAST_CHUNKS * CHUNK
MAX_REL = 128
MASK_VALUE = -1e30
D_FF = 2816
N_EXPERTS = 8
TOP_K = 2
N_PROJ_BLOCKS = 12

LANES = 128
SUBLANES = 8
TOKEN_TILE_ROWS = D_MODEL // LANES
VMEM_LIMIT_BYTES = 56 * 1024 * 1024


def _cparams(*semantics):
    return pltpu.CompilerParams(dimension_semantics=semantics,
                                vmem_limit_bytes=VMEM_LIMIT_BYTES)


def _split3(x):
    hi = x.astype(BF16)
    r1 = x - hi.astype(F32)
    mid = r1.astype(BF16)
    lo = (r1 - mid.astype(F32)).astype(BF16)
    return hi, mid, lo


def _dot(a, b):
    return jnp.dot(a, b, preferred_element_type=F32)


def _dot_nt(a, b):
    return lax.dot_general(a, b, (((1,), (1,)), ((), ())), preferred_element_type=F32)


def _dot_tn(a, b):
    return lax.dot_general(a, b, (((0,), (0,)), ((), ())), preferred_element_type=F32)


def _sigmoid(x):
    return 1.0 / (1.0 + jnp.exp(-x))


def _sigmoid_tanh(x):
    return 0.5 * jnp.tanh(0.5 * x) + 0.5


def _silu(x):
    half = 0.5 * x
    return half * (1.0 + jnp.tanh(half))


_Q_COL, _K_COL, _V_COL = 6, 7, 8
_F32_COLS = (0, 3)
_BF16_COLS = (1, 2, 4, 5, 9, 10, 11)
AX, BF = 0, 1
BQ, BV, BG, GB, AG, GA, GC = range(7)
_BF16_POS = {1: AG, 2: BQ, 4: BV, 5: BG, 9: GA, 10: GB, 11: GC}
_HGRN_POS_BLOCK = (BQ, 4)
_LRU_POS_BLOCK = (AG, 2)


def _head_indicator():
    e = np.zeros((D_MODEL, LANES), np.float32)
    e[np.arange(D_MODEL), np.arange(D_MODEL) // ATT_HEAD_DIM] = 1.0
    return e


def _in_proj_kernel(x_ref, g_ref, w_ref, qg_ref, kg_ref, e_ref, et_ref,
                    o32_ref, o16_ref, q16_ref, k16_ref, v16_ref, k32_ref, v32_ref, xn_ref,
                    *, tiles_per_tail, keep):
    i = pl.program_id(0)
    j = pl.program_id(1)
    tm = x_ref.shape[0]

    @pl.when(j == 0)
    def _():
        x = x_ref[...]
        ms = jnp.mean(x * x, axis=-1, keepdims=True)
        xn_ref[...] = (x * lax.rsqrt(ms + EPS) * g_ref[...]).astype(BF16)

    acc = lambda: _dot(xn_ref[...], w_ref[...])
    is_tail = (i % tiles_per_tail) == tiles_per_tail - 1

    def head_norm(x, gain):
        ms = _dot((x * x).astype(BF16), e_ref[...]) * (1.0 / ATT_HEAD_DIM)
        rhi, rmid, _ = _split3(lax.rsqrt(ms + EPS))
        return x * (_dot(rhi, et_ref[...]) + _dot(rmid, et_ref[...])) * gain

    def is_one_of(cols):
        hit = j == cols[0]
        for c in cols[1:]:
            hit = hit | (j == c)
        return hit

    @pl.when(is_one_of(_F32_COLS))
    def _():
        o32_ref[...] = acc()

    @pl.when(is_one_of(_BF16_COLS))
    def _():
        o16_ref[...] = acc().astype(BF16)

    @pl.when(j == _Q_COL)
    def _():
        q16_ref[...] = (head_norm(acc(), qg_ref[...]) * ATT_SCALE).astype(BF16)

    @pl.when(j == _K_COL)
    def _():
        kn = head_norm(acc(), kg_ref[...])
        k16_ref[...] = kn.astype(BF16)

        @pl.when(is_tail)
        def _():
            k32_ref[...] = kn[tm - keep:, :]

    @pl.when(j == _V_COL)
    def _():
        v = acc()
        v16_ref[...] = v.astype(BF16)

        @pl.when(is_tail)
        def _():
            v32_ref[...] = v[tm - keep:, :]


def _in_proj(x, g, w, q_gain, k_gain, seq, keep):
    n, d = x.shape
    tn = d
    if keep == seq:
        tm = min(n, 512)
        tiles_per_tail, keep_rows = 1, tm
    else:
        tm = min(n, 1024)
        assert seq % tm == 0 and keep <= tm
        tiles_per_tail, keep_rows = seq // tm, keep
    n_tail = n // (tiles_per_tail * tm) * keep_rows
    e = _head_indicator()
    tile_gain = lambda gain: jnp.tile(gain, N_ATT_HEADS).reshape(1, d)
    def stored_at(cols, positions):
        def index_map(i, j):
            pos = positions[cols[0]]
            for c in cols[1:]:
                pos = jnp.where(j >= c, positions[c], pos)
            return (pos, i, 0)
        return index_map

    def written_at(col, tiles):
        first = (tiles - 1) * N_PROJ_BLOCKS + col
        return lambda i, j: (jnp.maximum(i * N_PROJ_BLOCKS + j - first, 0) // (tiles * N_PROJ_BLOCKS), 0)

    row16 = lambda col: pl.BlockSpec((tm, d), written_at(col, 1))
    tail = lambda col: pl.BlockSpec((keep_rows, d), written_at(col, tiles_per_tail))
    vec = lambda: pl.BlockSpec((1, d), lambda i, j: (0, 0))
    return pl.pallas_call(
        functools.partial(_in_proj_kernel, tiles_per_tail=tiles_per_tail, keep=keep_rows),
        grid=(n // tm, N_PROJ_BLOCKS),
        in_specs=[
            pl.BlockSpec((tm, d), lambda i, j: (i, 0)),
            vec(),
            pl.BlockSpec((d, tn), lambda i, j: (0, j)),
            vec(), vec(),
            pl.BlockSpec((d, LANES), lambda i, j: (0, 0)),
            pl.BlockSpec((LANES, d), lambda i, j: (0, 0)),
        ],
        out_specs=[pl.BlockSpec((None, tm, tn), stored_at(_F32_COLS, {0: AX, 3: BF})),
                   pl.BlockSpec((None, tm, tn), stored_at(_BF16_COLS, _BF16_POS)),
                   row16(_Q_COL), row16(_K_COL), row16(_V_COL), tail(_K_COL), tail(_V_COL)],
        out_shape=[jax.ShapeDtypeStruct((len(_F32_COLS), n, d), F32),
                   jax.ShapeDtypeStruct((len(_BF16_COLS), n, d), BF16)]
                  + [jax.ShapeDtypeStruct((n, d), BF16)] * 3
                  + [jax.ShapeDtypeStruct((n_tail, d), F32)] * 2,
        scratch_shapes=[pltpu.VMEM((tm, d), BF16)],
        compiler_params=_cparams("arbitrary", "arbitrary"),
        name="norm_in_proj",
    )(x, g.reshape(1, d), w, tile_gain(q_gain), tile_gain(k_gain),
      jnp.asarray(e, BF16), jnp.asarray(e.T, BF16))


_XP_PAD = SUBLANES


def _lru_kernel(ax_ref, gates_ref, conv0_ref, h0_ref, cw_ref, cb_ref, wri_ref, br_ref, bi_ref,
                lam_ref, y_ref, conv_out_ref, h_out_ref, xp_ref, a_ref, u_ref, h_ref, hc_ref):
    i = pl.program_id(1)
    tt = ax_ref.shape[0]
    keep = CONV_W - 1

    @pl.when(i == 0)
    def _():
        xp_ref[_XP_PAD - keep:_XP_PAD, :] = conv0_ref[0]
        hc_ref[...] = h0_ref[0]

    x = ax_ref[...]
    xp_ref[_XP_PAD:_XP_PAD + tt, :] = x
    conv = cb_ref[...] + cw_ref[CONV_W - 1:CONV_W, :] * x
    for j in range(CONV_W - 1):
        off = _XP_PAD - (CONV_W - 1 - j)
        conv = conv + cw_ref[j:j + 1, :] * xp_ref[off:off + tt, :]
    tail = xp_ref[_XP_PAD + tt - keep:_XP_PAD + tt, :]
    xp_ref[_XP_PAD - keep:_XP_PAD, :] = tail
    conv_out_ref[0] = tail

    cb16 = conv.astype(BF16)
    r_parts, i_parts = [], []
    for n in range(N_LRU_BLOCKS):
        g = _dot(cb16[:, n * LRU_BLOCK:(n + 1) * LRU_BLOCK], wri_ref[n])
        r_parts.append(g[:, :LRU_BLOCK])
        i_parts.append(g[:, LRU_BLOCK:])
    r = _sigmoid_tanh(jnp.concatenate(r_parts, axis=-1) + br_ref[...])
    ig = _sigmoid_tanh(jnp.concatenate(i_parts, axis=-1) + bi_ref[...])
    lam = lam_ref[...]
    softplus_neg = jnp.maximum(-lam, 0.0) + jnp.log1p(jnp.exp(-jnp.abs(lam)))
    log_a = -LRU_C * r * softplus_neg
    a = jnp.exp(log_a)
    th = jnp.tanh(log_a)
    u = jnp.sqrt(jnp.maximum(-2.0 * th / (1.0 - th), 0.0)) * ig * conv

    row = lax.broadcasted_iota(jnp.int32, a.shape, 0) % SUBLANES
    d = 1
    while d < SUBLANES:
        ok = row >= d
        a_sh = jnp.where(ok, pltpu.roll(a, d, 0), 1.0)
        u_sh = jnp.where(ok, pltpu.roll(u, d, 0), 0.0)
        u = a * u_sh + u
        a = a * a_sh
        d *= 2
    a_ref[...] = a
    u_ref[...] = u

    def group(gi, h):
        r0 = pl.multiple_of(gi * SUBLANES, SUBLANES)
        hh = a_ref[pl.ds(r0, SUBLANES), :] * h + u_ref[pl.ds(r0, SUBLANES), :]
        h_ref[pl.ds(r0, SUBLANES), :] = hh
        return hh[SUBLANES - 1:SUBLANES, :]

    h_last = lax.fori_loop(0, tt // SUBLANES, group, hc_ref[...], unroll=4)
    hc_ref[...] = h_last
    h_out_ref[0] = h_last
    y_ref[...] = (_sigmoid_tanh(gates_ref[GA - AG].astype(F32)) * h_ref[...]
                  * jax.nn.gelu(gates_ref[0].astype(F32))).astype(BF16)


def _lru_branch(proj32, proj16, conv0, h0, conv_w, conv_b, wri, b_r, b_i, lam, batch, seq):
    d = D_MODEL
    tt = min(seq, 512)
    nt = seq // tt
    row = lambda b, i: b * nt + i
    vec = lambda: pl.BlockSpec((1, d), lambda b, i: (0, 0))
    first, count = _LRU_POS_BLOCK
    return pl.pallas_call(
        _lru_kernel,
        grid=(batch, nt),
        in_specs=[
            pl.BlockSpec((None, tt, d), lambda b, i: (AX, row(b, i), 0)),
            pl.BlockSpec((count, tt, d), lambda b, i: (first // count, row(b, i), 0)),
            pl.BlockSpec((1, CONV_W - 1, d), lambda b, i: (b, 0, 0)),
            pl.BlockSpec((1, 1, d), lambda b, i: (b, 0, 0)),
            pl.BlockSpec((CONV_W, d), lambda b, i: (0, 0)),
            vec(),
            pl.BlockSpec((N_LRU_BLOCKS, LRU_BLOCK, 2 * LRU_BLOCK), lambda b, i: (0, 0, 0)),
            vec(), vec(), vec(),
        ],
        out_specs=[
            pl.BlockSpec((tt, d), lambda b, i: (row(b, i), 0)),
            pl.BlockSpec((1, CONV_W - 1, d), lambda b, i: (b, 0, 0)),
            pl.BlockSpec((1, 1, d), lambda b, i: (b, 0, 0)),
        ],
        out_shape=[
            jax.ShapeDtypeStruct((batch * seq, d), BF16),
            jax.ShapeDtypeStruct((batch, CONV_W - 1, d), F32),
            jax.ShapeDtypeStruct((batch, 1, d), F32),
        ],
        scratch_shapes=[
            pltpu.VMEM((_XP_PAD + tt, d), F32),
            pltpu.VMEM((tt, d), F32),
            pltpu.VMEM((tt, d), F32),
            pltpu.VMEM((tt, d), F32),
            pltpu.VMEM((1, d), F32),
        ],
        compiler_params=_cparams("parallel", "arbitrary"),
        name="lru_branch",
    )(proj32, proj16, conv0, h0.reshape(batch, 1, d), conv_w, conv_b.reshape(1, d), wri,
      b_r.reshape(1, d), b_i.reshape(1, d), lam.reshape(1, d))


_HGRN_BLOCK = 128
_HGRN_BLOCKS_PER_STEP = 4


def _hgrn_levels(block):
    return tuple(block >> (i + 1) for i in range(block.bit_length() - 1))


def _hgrn_masks(block):
    t = np.arange(block)[:, None]
    s = np.arange(block)[None, :]
    masks = []
    for g in _hgrn_levels(block):
        masks.append((t // (2 * g) == s // (2 * g)) & (t % (2 * g) >= g) & (s % (2 * g) < g))
    masks.append(t == s)
    return np.stack(masks).astype(np.float32)


def _hgrn_kernel(p16_ref, f_ref, s0_ref, lb_ref, gn_ref, tri_ref, mask_ref,
                 y_ref, s_out_ref, st_ref, b_ref, *, block, blocks_per_step, n_steps):
    step = pl.program_id(1)

    def load_state():
        for h in range(N_HGRN_HEADS):
            st_ref[h] = s0_ref[0, h].T

    def store_state():
        for h in range(N_HGRN_HEADS):
            s_out_ref[0, h] = st_ref[h].T

    if n_steps == 1:
        load_state()
    else:
        pl.when(step == 0)(load_state)

    lb = lb_ref[...]
    gn = gn_ref[...]
    tri = tri_ref[...]
    rowi = lax.broadcasted_iota(jnp.int32, (block, D_MODEL), 0)

    for ci in range(blocks_per_step):
        rows = slice(ci * block, (ci + 1) * block)
        th = jnp.tanh(0.5 * f_ref[rows, :])
        f = lb + (1.0 - lb) * (0.5 + 0.5 * th)
        kb = (1.0 - lb) * (0.5 - 0.5 * th)
        qs = _silu(p16_ref[BQ, rows, :].astype(F32))
        v16 = p16_ref[BV, rows, :]

        hi, mid, lo = _split3(jnp.log2(jnp.maximum(f, MIN_FORGET)))
        b = _dot(tri, hi) + _dot(tri, mid) + _dot(tri, lo)
        bc_ref = b_ref.at[ci]
        bc_ref[...] = b
        b_last = bc_ref[block - 1:block, :]

        q_inter = (qs * jnp.exp2(b)).astype(BF16)
        k_state = (kb * jnp.exp2(b_last - b)).astype(BF16)

        q_lv, k_lv = [], []
        for g in _hgrn_levels(block):
            if g >= 4:
                blocks = [jnp.broadcast_to(bc_ref[m * 2 * g + g - 1:m * 2 * g + g, :], (2 * g, D_MODEL))
                          for m in range(block // (2 * g))]
                ref = jnp.concatenate(blocks, axis=0) if len(blocks) > 1 else blocks[0]
            elif g == 2:
                p = rowi % 4
                ref = jnp.where(p == 0, pltpu.roll(b, block - 1, 0),
                                jnp.where(p == 1, b, jnp.where(p == 2, pltpu.roll(b, 1, 0),
                                                               pltpu.roll(b, 2, 0))))
            else:
                ref = jnp.where(rowi % 2 == 0, b, pltpu.roll(b, 1, 0))
            fac = jnp.exp2(-jnp.abs(b - ref))
            q_lv.append((qs * fac).astype(BF16))
            k_lv.append((kb * fac).astype(BF16))
        q_lv.append(qs.astype(BF16))
        k_lv.append(kb.astype(BF16))

        gate = (_silu(p16_ref[BG, rows, :].astype(F32))
                * _sigmoid_tanh(p16_ref[GB, rows, :].astype(F32)))
        for h in range(N_HGRN_HEADS):
            sl = slice(h * HGRN_HEAD_DIM, (h + 1) * HGRN_HEAD_DIM)
            st = st_ref[h]
            att = mask_ref[0] * _dot_nt(q_lv[0][:, sl], k_lv[0][:, sl])
            for l in range(1, len(q_lv)):
                att = att + mask_ref[l] * _dot_nt(q_lv[l][:, sl], k_lv[l][:, sl])
            o = _dot_nt(q_inter[:, sl], st.astype(BF16)) + _dot(att.astype(BF16), v16[:, sl])
            ms = jnp.mean(o * o, axis=-1, keepdims=True)
            y_ref[rows, sl] = (o * lax.rsqrt(ms + EPS) * gn[:, sl] * gate[:, sl]).astype(BF16)
            st_ref[h] = st * jnp.exp2(b_last[:, sl]) + _dot_tn(v16[:, sl], k_state[:, sl])

    if n_steps == 1:
        store_state()
    else:
        pl.when(step == n_steps - 1)(store_state)


def _hgrn_branch(proj32, proj16, s0, lb, gn, batch, seq):
    d = D_MODEL
    block = min(seq, _HGRN_BLOCK)
    bps = min(seq // block, _HGRN_BLOCKS_PER_STEP)
    tt = bps * block
    nt = seq // tt
    row = lambda b, c: b * nt + c
    tri = jnp.asarray(np.tril(np.ones((block, block), np.float32)), BF16)
    masks = jnp.asarray(_hgrn_masks(block))
    first, count = _HGRN_POS_BLOCK
    state = lambda: pl.BlockSpec((1, N_HGRN_HEADS, HGRN_HEAD_DIM, HGRN_HEAD_DIM),
                                 lambda b, c: (b, 0, 0, 0))
    vec = lambda: pl.BlockSpec((1, d), lambda b, c: (0, 0))
    return pl.pallas_call(
        functools.partial(_hgrn_kernel, block=block, blocks_per_step=bps, n_steps=nt),
        grid=(batch, nt),
        in_specs=[pl.BlockSpec((count, tt, d), lambda b, c: (first // count, row(b, c), 0)),
                  pl.BlockSpec((None, tt, d), lambda b, c: (BF, row(b, c), 0)),
                  state(), vec(), vec(),
                  pl.BlockSpec((block, block), lambda b, c: (0, 0)),
                  pl.BlockSpec(masks.shape, lambda b, c: (0, 0, 0))],
        out_specs=[pl.BlockSpec((tt, d), lambda b, c: (row(b, c), 0)), state()],
        out_shape=[jax.ShapeDtypeStruct((batch * seq, d), BF16),
                   jax.ShapeDtypeStruct(s0.shape, F32)],
        scratch_shapes=[pltpu.VMEM((N_HGRN_HEADS, HGRN_HEAD_DIM, HGRN_HEAD_DIM), F32),
                        pltpu.VMEM((bps, block, d), F32)],
        compiler_params=_cparams("parallel", "arbitrary"),
        name="hgrn_branch",
    )(proj16, proj32, s0, lb.reshape(1, d), gn.reshape(1, d), tri, masks)


_ATT_PAIRS_PER_STEP = 8
_ATT_LANES = _ATT_PAIRS_PER_STEP * LANES
_ATT_STEPS = N_HEAD_PAIRS // _ATT_PAIRS_PER_STEP


def _attn_kernel(*refs, n_kv, dynamic_valid, tq):
    q_ref = refs[0]
    k_refs = refs[1:1 + n_kv]
    v_refs = refs[1 + n_kv:1 + 2 * n_kv]
    bias_ref = refs[1 + 2 * n_kv]
    gc_ref = refs[2 + 2 * n_kv]
    o_ref = refs[3 + 2 * n_kv]

    lane = lax.broadcasted_iota(jnp.int32, (tq, LANES), 1)
    lo_half = lane < ATT_HEAD_DIM
    for pi in range(q_ref.shape[1] // LANES):
        ls = slice(pi * LANES, (pi + 1) * LANES)
        q = q_ref[:, ls]
        k = jnp.concatenate([r[:, ls] for r in k_refs], axis=0) if n_kv > 1 else k_refs[0][:, ls]
        v = jnp.concatenate([r[:, ls] for r in v_refs], axis=0) if n_kv > 1 else v_refs[0][:, ls]
        zero = jnp.zeros_like(q)
        outs = []
        for hh, qm in enumerate((jnp.where(lo_half, q, zero), jnp.where(lo_half, zero, q))):
            s = _dot_nt(qm, k) + bias_ref[2 * pi + hh]
            if dynamic_valid:
                first_valid = BAND_PAST - pl.program_id(2) * tq
                colj = lax.broadcasted_iota(jnp.int32, s.shape, 1)
                s = jnp.where(colj >= first_valid, s, MASK_VALUE)
            m = jnp.max(s, axis=-1, keepdims=True)
            p = jnp.exp(s - m)
            denom = jnp.sum(p, axis=-1, keepdims=True)
            outs.append(_dot(p.astype(BF16), v) / denom)
        o_ref[:, ls] = (jnp.where(lo_half, outs[0], outs[1])
                        * _sigmoid_tanh(gc_ref[:, ls].astype(F32))).astype(BF16)


def _band_bias(table, tq, n_keys, banded):
    t = np.arange(tq)[:, None]
    j = np.arange(n_keys)[None, :]
    width = tq + n_keys - 1
    dist = BAND_PAST + (tq - 1) - np.arange(width + 1)
    diag = table[:, np.clip(dist, -MAX_REL, MAX_REL) + MAX_REL]
    skew = jnp.tile(diag, (1, tq))[:, :tq * width].reshape(-1, tq, width)
    bias = skew[:, :, tq - 1:tq - 1 + n_keys]
    if banded:
        c0 = (t // CHUNK) * CHUNK
        in_band = (j >= c0) & (j < c0 + BAND_PAST + CHUNK)
        bias = jnp.where(jnp.asarray(in_band)[None], bias, MASK_VALUE)
    return bias


def _attention_prompt(q16, k16, v16, proj16, table, batch, seq):
    tq = 256
    nt = seq // tq
    n_kv = BAND_PAST // tq + 1
    bias = _band_bias(table, tq, n_kv * tq, True)
    kv = lambda back: pl.BlockSpec(
        (tq, _ATT_LANES), lambda p, b, i: (b * nt + jnp.maximum(i - back, 0), p))
    kv_specs = [kv(n_kv - 1 - j) for j in range(n_kv)]
    return pl.pallas_call(
        functools.partial(_attn_kernel, n_kv=n_kv, dynamic_valid=True, tq=tq),
        grid=(_ATT_STEPS, batch, nt),
        in_specs=[pl.BlockSpec((tq, _ATT_LANES), lambda p, b, i: (b * nt + i, p))] + kv_specs + kv_specs
                 + [pl.BlockSpec((2 * _ATT_PAIRS_PER_STEP, tq, n_kv * tq), lambda p, b, i: (p, 0, 0)),
                    pl.BlockSpec((None, tq, _ATT_LANES), lambda p, b, i: (GC, b * nt + i, p))],
        out_specs=pl.BlockSpec((tq, _ATT_LANES), lambda p, b, i: (b * nt + i, p)),
        out_shape=jax.ShapeDtypeStruct((batch * seq, D_MODEL), BF16),
        compiler_params=_cparams("parallel", "parallel", "arbitrary"),
        name="band_attention_prompt",
    )(q16, *([k16] * n_kv), *([v16] * n_kv), bias, proj16)


def _attention_sample(q16, k16, v16, proj16, cache_k16, cache_v16, table, batch, seq):
    cl = cache_k16.shape[0] // batch
    bias = _band_bias(table, seq, cl + seq, False)
    own = pl.BlockSpec((seq, _ATT_LANES), lambda p, b: (b, p))
    cache = pl.BlockSpec((cl, _ATT_LANES), lambda p, b: (b, p))
    return pl.pallas_call(
        functools.partial(_attn_kernel, n_kv=2, dynamic_valid=False, tq=seq),
        grid=(_ATT_STEPS, batch),
        in_specs=[own, cache, own, cache, own,
                  pl.BlockSpec((2 * _ATT_PAIRS_PER_STEP, seq, cl + seq), lambda p, b: (p, 0, 0)),
                  pl.BlockSpec((None, seq, _ATT_LANES), lambda p, b: (GC, b, p))],
        out_specs=own,
        out_shape=jax.ShapeDtypeStruct((batch * seq, D_MODEL), BF16),
        compiler_params=_cparams("parallel", "parallel"),
        name="band_attention_sample",
    )(q16, cache_k16, k16, cache_v16, v16, bias, proj16)


def _store_token_tiles(ref, x):
    rows = x.shape[0]
    for s in range(TOKEN_TILE_ROWS):
        ref[pl.ds(s, rows, stride=TOKEN_TILE_ROWS), :] = x[:, s * LANES:(s + 1) * LANES]


def _load_token_tiles(ref, rows, s):
    return ref[pl.ds(s, rows, stride=TOKEN_TILE_ROWS), :]


def _merge_kernel(ya_ref, yb_ref, yc_ref, x_ref, w_ref, gn_ref, *rest,
                  with_router):
    if with_router:
        wr_ref, x1_ref, xn_ref, route_ref = rest
    else:
        x1_ref, xn_ref = rest
    merged = ya_ref[...].astype(F32) + yb_ref[...].astype(F32) + yc_ref[...].astype(F32)
    x1 = x_ref[...] + _dot(merged.astype(BF16), w_ref[...])
    x1_ref[...] = x1
    ms = jnp.mean(x1 * x1, axis=-1, keepdims=True)
    xn = x1 * lax.rsqrt(ms + EPS) * gn_ref[...]
    if not with_router:
        xn_ref[...] = xn.astype(BF16)
    else:
        _store_token_tiles(xn_ref, xn)
        xh, xl, _ = _split3(xn)
        wh = wr_ref[0]
        wl = wr_ref[1]
        logits = _dot(xh, wh) + (_dot(xh, wl) + _dot(xl, wh))
        lane = lax.broadcasted_iota(jnp.int32, logits.shape, 1).astype(F32)
        neg = jnp.float32(-jnp.inf)
        logits = jnp.where(lane < N_EXPERTS, logits, neg)
        m1 = jnp.max(logits, axis=-1, keepdims=True)
        i1 = jnp.min(jnp.where(logits == m1, lane, float(LANES)), axis=-1, keepdims=True)
        rest_l = jnp.where(lane == i1, neg, logits)
        m2 = jnp.max(rest_l, axis=-1, keepdims=True)
        i2 = jnp.min(jnp.where(rest_l == m2, lane, float(LANES)), axis=-1, keepdims=True)
        e2 = jnp.exp(m2 - m1)
        p1 = 1.0 / (1.0 + e2)
        p2 = e2 / (1.0 + e2)
        route_ref[...] = jnp.where(lane == 0.0, i1, jnp.where(lane == 1.0, i2,
                                   jnp.where(lane == 2.0, p1, jnp.where(lane == 3.0, p2, 0.0))))


def _merge_out(ya, yb, yc, x, w_out16, ffn_gain, router=None):
    n, d = x.shape
    tm = min(n, 256)
    rowspec = pl.BlockSpec((tm, d), lambda i: (i, 0))
    in_specs = [rowspec, rowspec, rowspec, rowspec,
                pl.BlockSpec((d, d), lambda i: (0, 0)),
                pl.BlockSpec((1, d), lambda i: (0, 0))]
    args = [ya, yb, yc, x, w_out16, ffn_gain.reshape(1, d)]
    out_specs = [rowspec, rowspec]
    out_shape = [jax.ShapeDtypeStruct((n, d), F32), jax.ShapeDtypeStruct((n, d), BF16)]
    if router is not None:
        out_specs[1] = pl.BlockSpec((tm * TOKEN_TILE_ROWS, LANES), lambda i: (i, 0))
        out_shape[1] = jax.ShapeDtypeStruct((n * TOKEN_TILE_ROWS, LANES), F32)
        wr = jnp.pad(router, ((0, 0), (0, LANES - N_EXPERTS)))
        wh = wr.astype(BF16)
        wl = (wr - wh.astype(F32)).astype(BF16)
        in_specs.append(pl.BlockSpec((2, d, LANES), lambda i: (0, 0, 0)))
        args.append(jnp.stack([wh, wl]))
        out_specs.append(pl.BlockSpec((tm, LANES), lambda i: (i, 0)))
        out_shape.append(jax.ShapeDtypeStruct((n, LANES), F32))
    return pl.pallas_call(
        functools.partial(_merge_kernel, with_router=router is not None),
        grid=(n // tm,),
        in_specs=in_specs,
        out_specs=out_specs,
        out_shape=out_shape,
        compiler_params=_cparams("parallel"),
        name="merge_out_proj",
    )(*args)


_FF_BLOCK = D_FF // 2
_N_FF_BLOCKS = D_FF // _FF_BLOCK


def _swiglu_block(xn16, wg, wu, wd):
    hg = _dot(xn16, wg)
    hu = _dot(xn16, wu)
    return _dot((_silu(hg) * hu).astype(BF16), wd)


def _ffn_kernel(xn_ref, x1_ref, wg_ref, wu_ref, wd_ref, o_ref, acc_ref):
    j = pl.program_id(1)
    y = _swiglu_block(xn_ref[...], wg_ref[...], wu_ref[...], wd_ref[...])

    @pl.when(j == 0)
    def _():
        acc_ref[...] = x1_ref[...] + y

    @pl.when(j > 0)
    def _():
        acc_ref[...] += y

    @pl.when(j == _N_FF_BLOCKS - 1)
    def _():
        o_ref[...] = acc_ref[...]


def _ffn(xn16, x1, wg16, wu16, wd16):
    n, d = x1.shape
    tm = min(n, 512)
    rowspec = lambda: pl.BlockSpec((tm, d), lambda i, j: (i, 0))
    return pl.pallas_call(
        _ffn_kernel,
        grid=(n // tm, _N_FF_BLOCKS),
        in_specs=[rowspec(), rowspec(),
                  pl.BlockSpec((d, _FF_BLOCK), lambda i, j: (0, j)),
                  pl.BlockSpec((d, _FF_BLOCK), lambda i, j: (0, j)),
                  pl.BlockSpec((_FF_BLOCK, d), lambda i, j: (j, 0))],
        out_specs=rowspec(),
        out_shape=jax.ShapeDtypeStruct((n, d), F32),
        scratch_shapes=[pltpu.VMEM((tm, d), F32)],
        compiler_params=_cparams("parallel", "arbitrary"),
        name="swiglu_ffn",
    )(xn16, x1, wg16, wu16, wd16)


_DISPATCH_CHUNK = 2048
_COMBINE_TOKENS = 256


def _routing_tables(route, n, tm):
    experts = route[:, :TOP_K].astype(jnp.int32).reshape(-1)
    onehot = (jnp.arange(N_EXPERTS, dtype=jnp.int32)[:, None] == experts[None]).astype(jnp.int32)
    running = jnp.cumsum(onehot, axis=1)
    rank = jnp.sum(running * onehot, axis=0) - 1
    count = running[:, -1]
    group = (count + tm - 1) // tm * tm
    group_end = jnp.cumsum(group)
    group_start = group_end - group
    dest = jnp.sum(onehot * group_start[:, None], axis=0) + rank
    n_tiles = TOP_K * n // tm + N_EXPERTS
    tile_start = jnp.arange(n_tiles, dtype=jnp.int32) * tm
    tile_expert = jnp.minimum(jnp.sum(tile_start[:, None] >= group_end[None], axis=1), N_EXPERTS - 1)
    n_used = (group_end[-1] // tm).reshape(1)
    return dest.astype(jnp.int32), tile_expert.astype(jnp.int32), n_used.astype(jnp.int32)


def _token_rows(index):
    return pl.ds(pl.multiple_of(index * TOKEN_TILE_ROWS, TOKEN_TILE_ROWS), TOKEN_TILE_ROWS)


def _dispatch_kernel(dest_ref, src_ref, init_ref, dst_ref, sem):
    del init_ref
    chunk = dest_ref.shape[2]

    def issue(token, carry):
        for k in range(TOP_K):
            pltpu.make_async_copy(src_ref.at[_token_rows(token)],
                                  dst_ref.at[_token_rows(dest_ref[0, 0, TOP_K * token + k])],
                                  sem).start()
        return carry

    lax.fori_loop(0, chunk // TOP_K, issue, 0)
    for k in range(TOP_K):
        pltpu.make_async_copy(src_ref, dst_ref.at[pl.ds(0, src_ref.shape[0])], sem).wait()


def _dispatch(xn_tiles, dest, n_slots):
    n_assign = dest.shape[0]
    chunk = min(n_assign, _DISPATCH_CHUNK)
    steps = n_assign // chunk
    shape = jax.ShapeDtypeStruct((n_slots * TOKEN_TILE_ROWS, LANES), F32)
    return pl.pallas_call(
        _dispatch_kernel,
        grid=(steps,),
        in_specs=[pl.BlockSpec((1, 1, chunk), lambda i: (i, 0, 0), memory_space=pltpu.SMEM),
                  pl.BlockSpec((chunk // TOP_K * TOKEN_TILE_ROWS, LANES), lambda i: (i, 0)),
                  pl.BlockSpec(memory_space=pl.ANY)],
        out_specs=pl.BlockSpec(memory_space=pl.ANY),
        out_shape=shape,
        scratch_shapes=[pltpu.SemaphoreType.DMA(())],
        input_output_aliases={2: 0},
        compiler_params=_cparams("arbitrary"),
        name="moe_dispatch",
    )(dest.reshape(steps, 1, chunk), xn_tiles, jnp.zeros(shape.shape, F32))


def _expert_ffn_kernel(te_ref, nu_ref, x_ref, wg_ref, wu_ref, wd_ref, y_ref, xs_ref, acc_ref):
    del te_ref
    i = pl.program_id(0)
    j = pl.program_id(1)
    tm = xs_ref.shape[0]
    used = i < nu_ref[0]

    @pl.when(used & (j == 0))
    def _():
        for s in range(TOKEN_TILE_ROWS):
            xs_ref[:, s * LANES:(s + 1) * LANES] = _load_token_tiles(x_ref, tm, s).astype(BF16)

    @pl.when(used)
    def _():
        y = _swiglu_block(xs_ref[...], wg_ref[0], wu_ref[0], wd_ref[0])

        @pl.when(j == 0)
        def _():
            acc_ref[...] = y

        @pl.when(j > 0)
        def _():
            acc_ref[...] += y

    @pl.when(used & (j == _N_FF_BLOCKS - 1))
    def _():
        _store_token_tiles(y_ref, acc_ref[...])

    @pl.when(jnp.logical_not(used) & (j == _N_FF_BLOCKS - 1))
    def _():
        y_ref[...] = jnp.zeros(y_ref.shape, F32)


def _expert_ffn(x_sorted, tile_expert, n_used, wg16, wu16, wd16, tm):
    d = D_MODEL
    n_tiles = tile_expert.shape[0]
    rows = pl.BlockSpec((tm * TOKEN_TILE_ROWS, LANES), lambda i, j, te, nu: (i, 0))
    grid_spec = pltpu.PrefetchScalarGridSpec(
        num_scalar_prefetch=2,
        grid=(n_tiles, _N_FF_BLOCKS),
        in_specs=[rows,
                  pl.BlockSpec((1, d, _FF_BLOCK), lambda i, j, te, nu: (te[i], 0, j)),
                  pl.BlockSpec((1, d, _FF_BLOCK), lambda i, j, te, nu: (te[i], 0, j)),
                  pl.BlockSpec((1, _FF_BLOCK, d), lambda i, j, te, nu: (te[i], j, 0))],
        out_specs=rows,
        scratch_shapes=[pltpu.VMEM((tm, d), BF16), pltpu.VMEM((tm, d), F32)])
    return pl.pallas_call(
        _expert_ffn_kernel,
        grid_spec=grid_spec,
        out_shape=jax.ShapeDtypeStruct(x_sorted.shape, F32),
        compiler_params=_cparams("arbitrary", "arbitrary"),
        name="moe_expert_ffn",
    )(tile_expert, n_used, x_sorted, wg16, wu16, wd16)


def _combine_kernel(dest_ref, next_dest_ref, x1_ref, route_ref, y_ref, o_ref, buf_ref, sem):
    i = pl.program_id(0)
    last = pl.num_programs(0) - 1
    tc = x1_ref.shape[0]

    def copies(idx_ref, slot, t, k):
        return pltpu.make_async_copy(y_ref.at[_token_rows(idx_ref[0, 0, TOP_K * t + k])],
                                     buf_ref.at[slot, k, _token_rows(t)], sem.at[slot])

    def issue(idx_ref, slot):
        def body(t, carry):
            for k in range(TOP_K):
                copies(idx_ref, slot, t, k).start()
            return carry
        lax.fori_loop(0, tc, body, 0)

    def finish(slot):
        for k in range(TOP_K):
            pltpu.make_async_copy(y_ref.at[pl.ds(0, tc * TOKEN_TILE_ROWS)], buf_ref.at[slot, k],
                                  sem.at[slot]).wait()
        route = route_ref[...]
        p1 = route[:, TOP_K:TOP_K + 1]
        p2 = route[:, TOP_K + 1:TOP_K + 2]
        for s in range(TOKEN_TILE_ROWS):
            sl = slice(s * LANES, (s + 1) * LANES)
            o_ref[:, sl] = (x1_ref[:, sl] + p1 * _load_token_tiles(buf_ref.at[slot, 0], tc, s)
                            + p2 * _load_token_tiles(buf_ref.at[slot, 1], tc, s))

    @pl.when(i == 0)
    def _():
        issue(dest_ref, 0)

    for slot in range(2):
        @pl.when(i % 2 == slot)
        def _(slot=slot):
            @pl.when(i < last)
            def _():
                issue(next_dest_ref, 1 - slot)
            finish(slot)


def _combine(x1, route, y_sorted, dest):
    n, d = x1.shape
    tc = min(n, _COMBINE_TOKENS)
    steps = n // tc
    dest3 = dest.reshape(steps, 1, TOP_K * tc)
    smem = lambda index_map: pl.BlockSpec((1, 1, TOP_K * tc), index_map, memory_space=pltpu.SMEM)
    return pl.pallas_call(
        _combine_kernel,
        grid=(steps,),
        in_specs=[smem(lambda i: (i, 0, 0)),
                  smem(lambda i: (jnp.minimum(i + 1, steps - 1), 0, 0)),
                  pl.BlockSpec((tc, d), lambda i: (i, 0)),
                  pl.BlockSpec((tc, LANES), lambda i: (i, 0)),
                  pl.BlockSpec(memory_space=pl.ANY)],
        out_specs=pl.BlockSpec((tc, d), lambda i: (i, 0)),
        out_shape=jax.ShapeDtypeStruct((n, d), F32),
        scratch_shapes=[pltpu.VMEM((2, TOP_K, tc * TOKEN_TILE_ROWS, LANES), F32),
                        pltpu.SemaphoreType.DMA((2,))],
        compiler_params=_cparams("arbitrary"),
        name="moe_combine",
    )(dest3, dest3, x1, route, y_sorted)


def _moe(xn_tiles, x1, route, wg16, wu16, wd16):
    n = x1.shape[0]
    tm = 512 if n >= 8192 else 256
    dest, tile_expert, n_used = _routing_tables(route, n, tm)
    x_sorted = _dispatch(xn_tiles, dest, tile_expert.shape[0] * tm)
    y_sorted = _expert_ffn(x_sorted, tile_expert, n_used, wg16, wu16, wd16, tm)
    return _combine(x1, route, y_sorted, dest)


def _mixer(x, layer_w, conv0, h0, s0, cache, batch, seq):
    keep = seq if cache is not None else min(BAND_PAST, seq)
    proj32, proj16, q16, k16, v16, k_tail, v_tail = _in_proj(
        x, layer_w["norm_mix"], layer_w["w_in"], layer_w["q_norm"], layer_w["k_norm"], seq, keep)
    ya, conv_new, h_new = _lru_branch(proj32, proj16, conv0, h0, layer_w["conv_w"], layer_w["conv_b"],
                                      layer_w["wri"], layer_w["lru_br"], layer_w["lru_bi"],
                                      layer_w["lru_lambda"], batch, seq)
    yb, s_new = _hgrn_branch(proj32, proj16, s0, layer_w["lb"], layer_w["hgrn_norm"], batch, seq)
    if cache is None:
        yc = _attention_prompt(q16, k16, v16, proj16, layer_w["rel_table"], batch, seq)
    else:
        yc = _attention_sample(q16, k16, v16, proj16, cache[0], cache[1], layer_w["rel_table"], batch,
                               seq)
    heads = (batch, keep, N_ATT_HEADS, ATT_HEAD_DIM)
    states = (conv_new, h_new.reshape(batch, D_MODEL), s_new, k_tail.reshape(heads), v_tail.reshape(heads))
    return ya, yb, yc, states


def kernel(x_prompt, x_sample, state_conv, state_lru, state_hgrn, cache_k, cache_v, norm_mix, w_in, conv_w, conv_b, lru_wr, lru_br, lru_wi, lru_bi, lru_lambda, hgrn_gamma, hgrn_norm, q_norm, k_norm, rel_bias_table, w_out, norm_ffn, ffn_w_gate, ffn_w_up, ffn_w_down, moe_router, moe_w_gate, moe_w_up, moe_w_down):
    depth = w_in.shape[0]
    bp, sp, d = x_prompt.shape
    bs, ss, _ = x_sample.shape

    p = jax.nn.softmax(hgrn_gamma.astype(F32), axis=0)
    lb_all = jnp.cumsum(p, axis=0) - p

    xp = x_prompt.reshape(bp * sp, d)
    xs = x_sample.reshape(bs * ss, d)
    zero_conv = jnp.zeros((bp, CONV_W - 1, d), F32)
    zero_h = jnp.zeros((bp, d), F32)
    zero_s = jnp.zeros((bp,) + state_hgrn.shape[2:], F32)

    st_p, st_s = [], []
    for l in range(depth):
        layer_w = dict(
            norm_mix=norm_mix[l], w_in=w_in[l].astype(BF16), conv_w=conv_w[l], conv_b=conv_b[l],
            wri=jnp.concatenate([lru_wr[l], lru_wi[l]], axis=-1).astype(BF16),
            lru_br=lru_br[l], lru_bi=lru_bi[l], lru_lambda=lru_lambda[l], lb=lb_all[l],
            hgrn_norm=hgrn_norm[l], q_norm=q_norm[l], k_norm=k_norm[l],
            rel_table=rel_bias_table[l])
        w_out16 = w_out[l].astype(BF16)
        cl = cache_k.shape[2]
        cache = (cache_k[l].reshape(bs * cl, d).astype(BF16),
                 cache_v[l].reshape(bs * cl, d).astype(BF16))
        dense = l % 2 == 0
        if dense:
            wg16 = ffn_w_gate[l // 2].astype(BF16)
            wu16 = ffn_w_up[l // 2].astype(BF16)
            wd16 = ffn_w_down[l // 2].astype(BF16)
            router = None
        else:
            wg16 = moe_w_gate[l // 2].astype(BF16)
            wu16 = moe_w_up[l // 2].astype(BF16)
            wd16 = moe_w_down[l // 2].astype(BF16)
            router = moe_router[l // 2]

        new_x = []
        for x, conv0, h0, s0, cch, batch, seq, acc in (
                (xp, zero_conv, zero_h, zero_s, None, bp, sp, st_p),
                (xs, state_conv[l], state_lru[l], state_hgrn[l], cache, bs, ss, st_s)):
            ya, yb, yc, states = _mixer(x, layer_w, conv0, h0, s0, cch, batch, seq)
            outs = _merge_out(ya, yb, yc, x, w_out16, norm_ffn[l], router)
            if router is None:
                new_x.append(_ffn(outs[1], outs[0], wg16, wu16, wd16))
            else:
                new_x.append(_moe(outs[1], outs[0], outs[2], wg16, wu16, wd16))
            acc.append(states)
        xp, xs = new_x

    stack = lambda sts, i: jnp.stack([s[i] for s in sts])
    return (xp.reshape(bp, sp, d), xs.reshape(bs, ss, d),
            stack(st_p, 0), stack(st_p, 1), stack(st_p, 2), stack(st_p, 3), stack(st_p, 4),
            stack(st_s, 0), stack(st_s, 1), stack(st_s, 2), stack(st_s, 3), stack(st_s, 4))
```

```python
import functools

import numpy as np
import jax
import jax.numpy as jnp
from jax import lax
from jax.experimental import pallas as pl
from jax.experimental.pallas import tpu as pltpu

F32 = jnp.float32
BF16 = jnp.bfloat16

D_MODEL = 1024
CHUNK = 64
EPS = 1e-6
MIN_FORGET = 1e-20
N_LRU_BLOCKS = 8
LRU_BLOCK = D_MODEL // N_LRU_BLOCKS
CONV_W = 4
LRU_C = 8.0
HGRN_HEAD_DIM = 128
N_HGRN_HEADS = D_MODEL // HGRN_HEAD_DIM
ATT_HEAD_DIM = 64
N_ATT_HEADS = D_MODEL // ATT_HEAD_DIM
N_HEAD_PAIRS = N_ATT_HEADS // 2
ATT_SCALE = ATT_HEAD_DIM ** -0.5
PAST_CHUNKS = 8
BAND_PAST = PAST_CHUNKS * CHUNK
MAX_REL = 128
MASK_VALUE = -1e30
D_FF = 2816
N_EXPERTS = 8
TOP_K = 2
N_PROJ_BLOCKS = 12

LANES = 128
SUBLANES = 8
TOKEN_TILE_ROWS = D_MODEL // LANES
VMEM_LIMIT_BYTES = 56 * 1024 * 1024


def _cparams(*semantics):
    return pltpu.CompilerParams(dimension_semantics=semantics,
                                vmem_limit_bytes=VMEM_LIMIT_BYTES)


def _split3(x):
    hi = x.astype(BF16)
    r1 = x - hi.astype(F32)
    mid = r1.astype(BF16)
    lo = (r1 - mid.astype(F32)).astype(BF16)
    return hi, mid, lo


def _dot(a, b):
    return jnp.dot(a, b, preferred_element_type=F32)


def _dot_nt(a, b):
    return lax.dot_general(a, b, (((1,), (1,)), ((), ())), preferred_element_type=F32)


def _dot_tn(a, b):
    return lax.dot_general(a, b, (((0,), (0,)), ((), ())), preferred_element_type=F32)


def _sigmoid(x):
    return 1.0 / (1.0 + jnp.exp(-x))


def _sigmoid_tanh(x):
    return 0.5 * jnp.tanh(0.5 * x) + 0.5


def _silu(x):
    half = 0.5 * x
    return half * (1.0 + jnp.tanh(half))


_Q_COL, _K_COL, _V_COL = 6, 7, 8
_F32_COLS = (0, 3)
_BF16_COLS = (1, 2, 4, 5, 9, 10, 11)
AX, BF = 0, 1
BQ, BV, BG, GB, AG, GA, GC = range(7)
_BF16_POS = {1: AG, 2: BQ, 4: BV, 5: BG, 9: GA, 10: GB, 11: GC}
_HGRN_POS_BLOCK = (BQ, 4)
_LRU_POS_BLOCK = (AG, 2)


def _head_indicator():
    e = np.zeros((D_MODEL, LANES), np.float32)
    e[np.arange(D_MODEL), np.arange(D_MODEL) // ATT_HEAD_DIM] = 1.0
    return e


def _in_proj_kernel(x_ref, g_ref, w_ref, qg_ref, kg_ref, e_ref, et_ref,
                    o32_ref, o16_ref, q16_ref, k16_ref, v16_ref, k32_ref, v32_ref, xn_ref,
                    *, tiles_per_tail, keep):
    i = pl.program_id(0)
    j = pl.program_id(1)
    tm = x_ref.shape[0]

    @pl.when(j == 0)
    def _():
        x = x_ref[...]
        ms = jnp.mean(x * x, axis=-1, keepdims=True)
        xn_ref[...] = (x * lax.rsqrt(ms + EPS) * g_ref[...]).astype(BF16)

    acc = lambda: _dot(xn_ref[...], w_ref[...])
    is_tail = (i % tiles_per_tail) == tiles_per_tail - 1

    def head_norm(x, gain):
        ms = _dot((x * x).astype(BF16), e_ref[...]) * (1.0 / ATT_HEAD_DIM)
        rhi, rmid, _ = _split3(lax.rsqrt(ms + EPS))
        return x * (_dot(rhi, et_ref[...]) + _dot(rmid, et_ref[...])) * gain

    def is_one_of(cols):
        hit = j == cols[0]
        for c in cols[1:]:
            hit = hit | (j == c)
        return hit

    @pl.when(is_one_of(_F32_COLS))
    def _():
        o32_ref[...] = acc()

    @pl.when(is_one_of(_BF16_COLS))
    def _():
        o16_ref[...] = acc().astype(BF16)

    @pl.when(j == _Q_COL)
    def _():
        q16_ref[...] = (head_norm(acc(), qg_ref[...]) * ATT_SCALE).astype(BF16)

    @pl.when(j == _K_COL)
    def _():
        kn = head_norm(acc(), kg_ref[...])
        k16_ref[...] = kn.astype(BF16)

        @pl.when(is_tail)
        def _():
            k32_ref[...] = kn[tm - keep:, :]

    @pl.when(j == _V_COL)
    def _():
        v = acc()
        v16_ref[...] = v.astype(BF16)

        @pl.when(is_tail)
        def _():
            v32_ref[...] = v[tm - keep:, :]


def _in_proj(x, g, w, q_gain, k_gain, seq, keep):
    n, d = x.shape
    tn = d
    if keep == seq:
        tm = min(n, 512)
        tiles_per_tail, keep_rows = 1, tm
    else:
        tm = min(n, 1024)
        assert seq % tm == 0 and keep <= tm
        tiles_per_tail, keep_rows = seq // tm, keep
    n_tail = n // (tiles_per_tail * tm) * keep_rows
    e = _head_indicator()
    tile_gain = lambda gain: jnp.tile(gain, N_ATT_HEADS).reshape(1, d)
    def stored_at(cols, positions):
        def index_map(i, j):
            pos = positions[cols[0]]
            for c in cols[1:]:
                pos = jnp.where(j >= c, positions[c], pos)
            return (pos, i, 0)
        return index_map

    def written_at(col, tiles):
        first = (tiles - 1) * N_PROJ_BLOCKS + col
        return lambda i, j: (jnp.maximum(i * N_PROJ_BLOCKS + j - first, 0) // (tiles * N_PROJ_BLOCKS), 0)

    row16 = lambda col: pl.BlockSpec((tm, d), written_at(col, 1))
    tail = lambda col: pl.BlockSpec((keep_rows, d), written_at(col, tiles_per_tail))
    vec = lambda: pl.BlockSpec((1, d), lambda i, j: (0, 0))
    return pl.pallas_call(
        functools.partial(_in_proj_kernel, tiles_per_tail=tiles_per_tail, keep=keep_rows),
        grid=(n // tm, N_PROJ_BLOCKS),
        in_specs=[
            pl.BlockSpec((tm, d), lambda i, j: (i, 0)),
            vec(),
            pl.BlockSpec((d, tn), lambda i, j: (0, j)),
            vec(), vec(),
            pl.BlockSpec((d, LANES), lambda i, j: (0, 0)),
            pl.BlockSpec((LANES, d), lambda i, j: (0, 0)),
        ],
        out_specs=[pl.BlockSpec((None, tm, tn), stored_at(_F32_COLS, {0: AX, 3: BF})),
                   pl.BlockSpec((None, tm, tn), stored_at(_BF16_COLS, _BF16_POS)),
                   row16(_Q_COL), row16(_K_COL), row16(_V_COL), tail(_K_COL), tail(_V_COL)],
        out_shape=[jax.ShapeDtypeStruct((len(_F32_COLS), n, d), F32),
                   jax.ShapeDtypeStruct((len(_BF16_COLS), n, d), BF16)]
                  + [jax.ShapeDtypeStruct((n, d), BF16)] * 3
                  + [jax.ShapeDtypeStruct((n_tail, d), F32)] * 2,
        scratch_shapes=[pltpu.VMEM((tm, d), BF16)],
        compiler_params=_cparams("arbitrary", "arbitrary"),
        name="norm_in_proj",
    )(x, g.reshape(1, d), w, tile_gain(q_gain), tile_gain(k_gain),
      jnp.asarray(e, BF16), jnp.asarray(e.T, BF16))


_XP_PAD = SUBLANES


def _lru_kernel(ax_ref, gates_ref, conv0_ref, h0_ref, cw_ref, cb_ref, wri_ref, br_ref, bi_ref,
                lam_ref, y_ref, conv_out_ref, h_out_ref, xp_ref, a_ref, u_ref, h_ref, hc_ref):
    i = pl.program_id(1)
    tt = ax_ref.shape[0]
    keep = CONV_W - 1

    @pl.when(i == 0)
    def _():
        xp_ref[_XP_PAD - keep:_XP_PAD, :] = conv0_ref[0]
        hc_ref[...] = h0_ref[0]

    x = ax_ref[...]
    xp_ref[_XP_PAD:_XP_PAD + tt, :] = x
    conv = cb_ref[...] + cw_ref[CONV_W - 1:CONV_W, :] * x
    for j in range(CONV_W - 1):
        off = _XP_PAD - (CONV_W - 1 - j)
        conv = conv + cw_ref[j:j + 1, :] * xp_ref[off:off + tt, :]
    tail = xp_ref[_XP_PAD + tt - keep:_XP_PAD + tt, :]
    xp_ref[_XP_PAD - keep:_XP_PAD, :] = tail
    conv_out_ref[0] = tail

    cb16 = conv.astype(BF16)
    r_parts, i_parts = [], []
    for n in range(N_LRU_BLOCKS):
        g = _dot(cb16[:, n * LRU_BLOCK:(n + 1) * LRU_BLOCK], wri_ref[n])
        r_parts.append(g[:, :LRU_BLOCK])
        i_parts.append(g[:, LRU_BLOCK:])
    r = _sigmoid_tanh(jnp.concatenate(r_parts, axis=-1) + br_ref[...])
    ig = _sigmoid_tanh(jnp.concatenate(i_parts, axis=-1) + bi_ref[...])
    lam = lam_ref[...]
    softplus_neg = jnp.maximum(-lam, 0.0) + jnp.log1p(jnp.exp(-jnp.abs(lam)))
    log_a = -LRU_C * r * softplus_neg
    a = jnp.exp(log_a)
    th = jnp.tanh(log_a)
    u = jnp.sqrt(jnp.maximum(-2.0 * th / (1.0 - th), 0.0)) * ig * conv

    row = lax.broadcasted_iota(jnp.int32, a.shape, 0) % SUBLANES
    d = 1
    while d < SUBLANES:
        ok = row >= d
        a_sh = jnp.where(ok, pltpu.roll(a, d, 0), 1.0)
        u_sh = jnp.where(ok, pltpu.roll(u, d, 0), 0.0)
        u = a * u_sh + u
        a = a * a_sh
        d *= 2
    a_ref[...] = a
    u_ref[...] = u

    def group(gi, h):
        r0 = pl.multiple_of(gi * SUBLANES, SUBLANES)
        hh = a_ref[pl.ds(r0, SUBLANES), :] * h + u_ref[pl.ds(r0, SUBLANES), :]
        h_ref[pl.ds(r0, SUBLANES), :] = hh
        return hh[SUBLANES - 1:SUBLANES, :]

    h_last = lax.fori_loop(0, tt // SUBLANES, group, hc_ref[...], unroll=4)
    hc_ref[...] = h_last
    h_out_ref[0] = h_last
    y_ref[...] = (_sigmoid_tanh(gates_ref[GA - AG].astype(F32)) * h_ref[...]
                  * jax.nn.gelu(gates_ref[0].astype(F32))).astype(BF16)


def _lru_branch(proj32, proj16, conv0, h0, conv_w, conv_b, wri, b_r, b_i, lam, batch, seq):
    d = D_MODEL
    tt = min(seq, 512)
    nt = seq // tt
    row = lambda b, i: b * nt + i
    vec = lambda: pl.BlockSpec((1, d), lambda b, i: (0, 0))
    first, count = _LRU_POS_BLOCK
    return pl.pallas_call(
        _lru_kernel,
        grid=(batch, nt),
        in_specs=[
            pl.BlockSpec((None, tt, d), lambda b, i: (AX, row(b, i), 0)),
            pl.BlockSpec((count, tt, d), lambda b, i: (first // count, row(b, i), 0)),
            pl.BlockSpec((1, CONV_W - 1, d), lambda b, i: (b, 0, 0)),
            pl.BlockSpec((1, 1, d), lambda b, i: (b, 0, 0)),
            pl.BlockSpec((CONV_W, d), lambda b, i: (0, 0)),
            vec(),
            pl.BlockSpec((N_LRU_BLOCKS, LRU_BLOCK, 2 * LRU_BLOCK), lambda b, i: (0, 0, 0)),
            vec(), vec(), vec(),
        ],
        out_specs=[
            pl.BlockSpec((tt, d), lambda b, i: (row(b, i), 0)),
            pl.BlockSpec((1, CONV_W - 1, d), lambda b, i: (b, 0, 0)),
            pl.BlockSpec((1, 1, d), lambda b, i: (b, 0, 0)),
        ],
        out_shape=[
            jax.ShapeDtypeStruct((batch * seq, d), BF16),
            jax.ShapeDtypeStruct((batch, CONV_W - 1, d), F32),
            jax.ShapeDtypeStruct((batch, 1, d), F32),
        ],
        scratch_shapes=[
            pltpu.VMEM((_XP_PAD + tt, d), F32),
            pltpu.VMEM((tt, d), F32),
            pltpu.VMEM((tt, d), F32),
            pltpu.VMEM((tt, d), F32),
            pltpu.VMEM((1, d), F32),
        ],
        compiler_params=_cparams("parallel", "arbitrary"),
        name="lru_branch",
    )(proj32, proj16, conv0, h0.reshape(batch, 1, d), conv_w, conv_b.reshape(1, d), wri,
      b_r.reshape(1, d), b_i.reshape(1, d), lam.reshape(1, d))


_HGRN_BLOCK = 128
_HGRN_BLOCKS_PER_STEP = 4


def _hgrn_levels(block):
    return tuple(block >> (i + 1) for i in range(block.bit_length() - 1))


def _hgrn_masks(block):
    t = np.arange(block)[:, None]
    s = np.arange(block)[None, :]
    masks = []
    for g in _hgrn_levels(block):
        masks.append((t // (2 * g) == s // (2 * g)) & (t % (2 * g) >= g) & (s % (2 * g) < g))
    masks.append(t == s)
    return np.stack(masks).astype(np.float32)


def _hgrn_kernel(p16_ref, f_ref, s0_ref, lb_ref, gn_ref, tri_ref, mask_ref,
                 y_ref, s_out_ref, st_ref, b_ref, *, block, blocks_per_step, n_steps):
    step = pl.program_id(1)

    def load_state():
        for h in range(N_HGRN_HEADS):
            st_ref[h] = s0_ref[0, h].T

    def store_state():
        for h in range(N_HGRN_HEADS):
            s_out_ref[0, h] = st_ref[h].T

    if n_steps == 1:
        load_state()
    else:
        pl.when(step == 0)(load_state)

    lb = lb_ref[...]
    gn = gn_ref[...]
    tri = tri_ref[...]
    rowi = lax.broadcasted_iota(jnp.int32, (block, D_MODEL), 0)

    for ci in range(blocks_per_step):
        rows = slice(ci * block, (ci + 1) * block)
        th = jnp.tanh(0.5 * f_ref[rows, :])
        f = lb + (1.0 - lb) * (0.5 + 0.5 * th)
        kb = (1.0 - lb) * (0.5 - 0.5 * th)
        qs = _silu(p16_ref[BQ, rows, :].astype(F32))
        v16 = p16_ref[BV, rows, :]

        hi, mid, lo = _split3(jnp.log2(jnp.maximum(f, MIN_FORGET)))
        b = _dot(tri, hi) + _dot(tri, mid) + _dot(tri, lo)
        bc_ref = b_ref.at[ci]
        bc_ref[...] = b
        b_last = bc_ref[block - 1:block, :]

        q_inter = (qs * jnp.exp2(b)).astype(BF16)
        k_state = (kb * jnp.exp2(b_last - b)).astype(BF16)

        q_lv, k_lv = [], []
        for g in _hgrn_levels(block):
            if g >= 4:
                blocks = [jnp.broadcast_to(bc_ref[m * 2 * g + g - 1:m * 2 * g + g, :], (2 * g, D_MODEL))
                          for m in range(block // (2 * g))]
                ref = jnp.concatenate(blocks, axis=0) if len(blocks) > 1 else blocks[0]
            elif g == 2:
                p = rowi % 4
                ref = jnp.where(p == 0, pltpu.roll(b, block - 1, 0),
                                jnp.where(p == 1, b, jnp.where(p == 2, pltpu.roll(b, 1, 0),
                                                               pltpu.roll(b, 2, 0))))
            else:
                ref = jnp.where(rowi % 2 == 0, b, pltpu.roll(b, 1, 0))
            fac = jnp.exp2(-jnp.abs(b - ref))
            q_lv.append((qs * fac).astype(BF16))
            k_lv.append((kb * fac).astype(BF16))
        q_lv.append(qs.astype(BF16))
        k_lv.append(kb.astype(BF16))

        gate = (_silu(p16_ref[BG, rows, :].astype(F32))
                * _sigmoid_tanh(p16_ref[GB, rows, :].astype(F32)))
        for h in range(N_HGRN_HEADS):
            sl = slice(h * HGRN_HEAD_DIM, (h + 1) * HGRN_HEAD_DIM)
            st = st_ref[h]
            att = mask_ref[0] * _dot_nt(q_lv[0][:, sl], k_lv[0][:, sl])
            for l in range(1, len(q_lv)):
                att = att + mask_ref[l] * _dot_nt(q_lv[l][:, sl], k_lv[l][:, sl])
            o = _dot_nt(q_inter[:, sl], st.astype(BF16)) + _dot(att.astype(BF16), v16[:, sl])
            ms = jnp.mean(o * o, axis=-1, keepdims=True)
            y_ref[rows, sl] = (o * lax.rsqrt(ms + EPS) * gn[:, sl] * gate[:, sl]).astype(BF16)
            st_ref[h] = st * jnp.exp2(b_last[:, sl]) + _dot_tn(v16[:, sl], k_state[:, sl])

    if n_steps == 1:
        store_state()
    else:
        pl.when(step == n_steps - 1)(store_state)


def _hgrn_branch(proj32, proj16, s0, lb, gn, batch, seq):
    d = D_MODEL
    block = min(seq, _HGRN_BLOCK)
    bps = min(seq // block, _HGRN_BLOCKS_PER_STEP)
    tt = bps * block
    nt = seq // tt
    row = lambda b, c: b * nt + c
    tri = jnp.asarray(np.tril(np.ones((block, block), np.float32)), BF16)
    masks = jnp.asarray(_hgrn_masks(block))
    first, count = _HGRN_POS_BLOCK
    state = lambda: pl.BlockSpec((1, N_HGRN_HEADS, HGRN_HEAD_DIM, HGRN_HEAD_DIM),
                                 lambda b, c: (b, 0, 0, 0))
    vec = lambda: pl.BlockSpec((1, d), lambda b, c: (0, 0))
    return pl.pallas_call(
        functools.partial(_hgrn_kernel, block=block, blocks_per_step=bps, n_steps=nt),
        grid=(batch, nt),
        in_specs=[pl.BlockSpec((count, tt, d), lambda b, c: (first // count, row(b, c), 0)),
                  pl.BlockSpec((None, tt, d), lambda b, c: (BF, row(b, c), 0)),
                  state(), vec(), vec(),
                  pl.BlockSpec((block, block), lambda b, c: (0, 0)),
                  pl.BlockSpec(masks.shape, lambda b, c: (0, 0, 0))],
        out_specs=[pl.BlockSpec((tt, d), lambda b, c: (row(b, c), 0)), state()],
        out_shape=[jax.ShapeDtypeStruct((batch * seq, d), BF16),
                   jax.ShapeDtypeStruct(s0.shape, F32)],
        scratch_shapes=[pltpu.VMEM((N_HGRN_HEADS, HGRN_HEAD_DIM, HGRN_HEAD_DIM), F32),
                        pltpu.VMEM((bps, block, d), F32)],
        compiler_params=_cparams("parallel", "arbitrary"),
        name="hgrn_branch",
    )(proj16, proj32, s0, lb.reshape(1, d), gn.reshape(1, d), tri, masks)


_ATT_PAIRS_PER_STEP = 8
_ATT_LANES = _ATT_PAIRS_PER_STEP * LANES
_ATT_STEPS = N_HEAD_PAIRS // _ATT_PAIRS_PER_STEP


def _attn_kernel(*refs, n_kv, dynamic_valid, tq):
    q_ref = refs[0]
    k_refs = refs[1:1 + n_kv]
    v_refs = refs[1 + n_kv:1 + 2 * n_kv]
    bias_ref = refs[1 + 2 * n_kv]
    gc_ref = refs[2 + 2 * n_kv]
    o_ref = refs[3 + 2 * n_kv]

    lane = lax.broadcasted_iota(jnp.int32, (tq, LANES), 1)
    lo_half = lane < ATT_HEAD_DIM
    for pi in range(q_ref.shape[1] // LANES):
        ls = slice(pi * LANES, (pi + 1) * LANES)
        q = q_ref[:, ls]
        k = jnp.concatenate([r[:, ls] for r in k_refs], axis=0) if n_kv > 1 else k_refs[0][:, ls]
        v = jnp.concatenate([r[:, ls] for r in v_refs], axis=0) if n_kv > 1 else v_refs[0][:, ls]
        zero = jnp.zeros_like(q)
        outs = []
        for hh, qm in enumerate((jnp.where(lo_half, q, zero), jnp.where(lo_half, zero, q))):
            s = _dot_nt(qm, k) + bias_ref[2 * pi + hh]
            if dynamic_valid:
                first_valid = BAND_PAST - pl.program_id(2) * tq
                colj = lax.broadcasted_iota(jnp.int32, s.shape, 1)
                s = jnp.where(colj >= first_valid, s, MASK_VALUE)
            m = jnp.max(s, axis=-1, keepdims=True)
            p = jnp.exp(s - m)
            denom = jnp.sum(p, axis=-1, keepdims=True)
            outs.append(_dot(p.astype(BF16), v) / denom)
        o_ref[:, ls] = (jnp.where(lo_half, outs[0], outs[1])
                        * _sigmoid_tanh(gc_ref[:, ls].astype(F32))).astype(BF16)


def _band_bias(table, tq, n_keys, banded):
    t = np.arange(tq)[:, None]
    j = np.arange(n_keys)[None, :]
    width = tq + n_keys - 1
    dist = BAND_PAST + (tq - 1) - np.arange(width + 1)
    diag = table[:, np.clip(dist, -MAX_REL, MAX_REL) + MAX_REL]
    skew = jnp.tile(diag, (1, tq))[:, :tq * width].reshape(-1, tq, width)
    bias = skew[:, :, tq - 1:tq - 1 + n_keys]
    if banded:
        c0 = (t // CHUNK) * CHUNK
        in_band = (j >= c0) & (j < c0 + BAND_PAST + CHUNK)
        bias = jnp.where(jnp.asarray(in_band)[None], bias, MASK_VALUE)
    return bias


def _attention_prompt(q16, k16, v16, proj16, table, batch, seq):
    tq = 256
    nt = seq // tq
    n_kv = BAND_PAST // tq + 1
    bias = _band_bias(table, tq, n_kv * tq, True)
    kv = lambda back: pl.BlockSpec(
        (tq, _ATT_LANES), lambda p, b, i: (b * nt + jnp.maximum(i - back, 0), p))
    kv_specs = [kv(n_kv - 1 - j) for j in range(n_kv)]
    return pl.pallas_call(
        functools.partial(_attn_kernel, n_kv=n_kv, dynamic_valid=True, tq=tq),
        grid=(_ATT_STEPS, batch, nt),
        in_specs=[pl.BlockSpec((tq, _ATT_LANES), lambda p, b, i: (b * nt + i, p))] + kv_specs + kv_specs
                 + [pl.BlockSpec((2 * _ATT_PAIRS_PER_STEP, tq, n_kv * tq), lambda p, b, i: (p, 0, 0)),
                    pl.BlockSpec((None, tq, _ATT_LANES), lambda p, b, i: (GC, b * nt + i, p))],
        out_specs=pl.BlockSpec((tq, _ATT_LANES), lambda p, b, i: (b * nt + i, p)),
        out_shape=jax.ShapeDtypeStruct((batch * seq, D_MODEL), BF16),
        compiler_params=_cparams("parallel", "parallel", "arbitrary"),
        name="band_attention_prompt",
    )(q16, *([k16] * n_kv), *([v16] * n_kv), bias, proj16)


def _attention_sample(q16, k16, v16, proj16, cache_k16, cache_v16, table, batch, seq):
    cl = cache_k16.shape[0] // batch
    bias = _band_bias(table, seq, cl + seq, False)
    own = pl.BlockSpec((seq, _ATT_LANES), lambda p, b: (b, p))
    cache = pl.BlockSpec((cl, _ATT_LANES), lambda p, b: (b, p))
    return pl.pallas_call(
        functools.partial(_attn_kernel, n_kv=2, dynamic_valid=False, tq=seq),
        grid=(_ATT_STEPS, batch),
        in_specs=[own, cache, own, cache, own,
                  pl.BlockSpec((2 * _ATT_PAIRS_PER_STEP, seq, cl + seq), lambda p, b: (p, 0, 0)),
                  pl.BlockSpec((None, seq, _ATT_LANES), lambda p, b: (GC, b, p))],
        out_specs=own,
        out_shape=jax.ShapeDtypeStruct((batch * seq, D_MODEL), BF16),
        compiler_params=_cparams("parallel", "parallel"),
        name="band_attention_sample",
    )(q16, cache_k16, k16, cache_v16, v16, bias, proj16)


def _store_token_tiles(ref, x):
    rows = x.shape[0]
    for s in range(TOKEN_TILE_ROWS):
        ref[pl.ds(s, rows, stride=TOKEN_TILE_ROWS), :] = x[:, s * LANES:(s + 1) * LANES]


def _load_token_tiles(ref, rows, s):
    return ref[pl.ds(s, rows, stride=TOKEN_TILE_ROWS), :]


def _merge_kernel(ya_ref, yb_ref, yc_ref, x_ref, w_ref, gn_ref, *rest,
                  with_router):
    if with_router:
        wr_ref, x1_ref, xn_ref, route_ref = rest
    else:
        x1_ref, xn_ref = rest
    merged = ya_ref[...].astype(F32) + yb_ref[...].astype(F32) + yc_ref[...].astype(F32)
    x1 = x_ref[...] + _dot(merged.astype(BF16), w_ref[...])
    x1_ref[...] = x1
    ms = jnp.mean(x1 * x1, axis=-1, keepdims=True)
    xn = x1 * lax.rsqrt(ms + EPS) * gn_ref[...]
    if not with_router:
        xn_ref[...] = xn.astype(BF16)
    else:
        _store_token_tiles(xn_ref, xn)
        xh, xl, _ = _split3(xn)
        wh = wr_ref[0]
        wl = wr_ref[1]
        logits = _dot(xh, wh) + (_dot(xh, wl) + _dot(xl, wh))
        lane = lax.broadcasted_iota(jnp.int32, logits.shape, 1).astype(F32)
        neg = jnp.float32(-jnp.inf)
        logits = jnp.where(lane < N_EXPERTS, logits, neg)
        m1 = jnp.max(logits, axis=-1, keepdims=True)
        i1 = jnp.min(jnp.where(logits == m1, lane, float(LANES)), axis=-1, keepdims=True)
        rest_l = jnp.where(lane == i1, neg, logits)
        m2 = jnp.max(rest_l, axis=-1, keepdims=True)
        i2 = jnp.min(jnp.where(rest_l == m2, lane, float(LANES)), axis=-1, keepdims=True)
        e2 = jnp.exp(m2 - m1)
        p1 = 1.0 / (1.0 + e2)
        p2 = e2 / (1.0 + e2)
        route_ref[...] = jnp.where(lane == 0.0, i1, jnp.where(lane == 1.0, i2,
                                   jnp.where(lane == 2.0, p1, jnp.where(lane == 3.0, p2, 0.0))))


def _merge_out(ya, yb, yc, x, w_out16, ffn_gain, router=None):
    n, d = x.shape
    tm = min(n, 256)
    rowspec = pl.BlockSpec((tm, d), lambda i: (i, 0))
    in_specs = [rowspec, rowspec, rowspec, rowspec,
                pl.BlockSpec((d, d), lambda i: (0, 0)),
                pl.BlockSpec((1, d), lambda i: (0, 0))]
    args = [ya, yb, yc, x, w_out16, ffn_gain.reshape(1, d)]
    out_specs = [rowspec, rowspec]
    out_shape = [jax.ShapeDtypeStruct((n, d), F32), jax.ShapeDtypeStruct((n, d), BF16)]
    if router is not None:
        out_specs[1] = pl.BlockSpec((tm * TOKEN_TILE_ROWS, LANES), lambda i: (i, 0))
        out_shape[1] = jax.ShapeDtypeStruct((n * TOKEN_TILE_ROWS, LANES), F32)
        wr = jnp.pad(router, ((0, 0), (0, LANES - N_EXPERTS)))
        wh = wr.astype(BF16)
        wl = (wr - wh.astype(F32)).astype(BF16)
        in_specs.append(pl.BlockSpec((2, d, LANES), lambda i: (0, 0, 0)))
        args.append(jnp.stack([wh, wl]))
        out_specs.append(pl.BlockSpec((tm, LANES), lambda i: (i, 0)))
        out_shape.append(jax.ShapeDtypeStruct((n, LANES), F32))
    return pl.pallas_call(
        functools.partial(_merge_kernel, with_router=router is not None),
        grid=(n // tm,),
        in_specs=in_specs,
        out_specs=out_specs,
        out_shape=out_shape,
        compiler_params=_cparams("parallel"),
        name="merge_out_proj",
    )(*args)


_FF_BLOCK = D_FF // 2
_N_FF_BLOCKS = D_FF // _FF_BLOCK


def _swiglu_block(xn16, wg, wu, wd):
    hg = _dot(xn16, wg)
    hu = _dot(xn16, wu)
    return _dot((_silu(hg) * hu).astype(BF16), wd)


def _ffn_kernel(xn_ref, x1_ref, wg_ref, wu_ref, wd_ref, o_ref, acc_ref):
    j = pl.program_id(1)
    y = _swiglu_block(xn_ref[...], wg_ref[...], wu_ref[...], wd_ref[...])

    @pl.when(j == 0)
    def _():
        acc_ref[...] = x1_ref[...] + y

    @pl.when(j > 0)
    def _():
        acc_ref[...] += y

    @pl.when(j == _N_FF_BLOCKS - 1)
    def _():
        o_ref[...] = acc_ref[...]


def _ffn(xn16, x1, wg16, wu16, wd16):
    n, d = x1.shape
    tm = min(n, 512)
    rowspec = lambda: pl.BlockSpec((tm, d), lambda i, j: (i, 0))
    return pl.pallas_call(
        _ffn_kernel,
        grid=(n // tm, _N_FF_BLOCKS),
        in_specs=[rowspec(), rowspec(),
                  pl.BlockSpec((d, _FF_BLOCK), lambda i, j: (0, j)),
                  pl.BlockSpec((d, _FF_BLOCK), lambda i, j: (0, j)),
                  pl.BlockSpec((_FF_BLOCK, d), lambda i, j: (j, 0))],
        out_specs=rowspec(),
        out_shape=jax.ShapeDtypeStruct((n, d), F32),
        scratch_shapes=[pltpu.VMEM((tm, d), F32)],
        compiler_params=_cparams("parallel", "arbitrary"),
        name="swiglu_ffn",
    )(xn16, x1, wg16, wu16, wd16)


_DISPATCH_CHUNK = 2048
_COMBINE_TOKENS = 256
_DMA_ISSUE_UNROLL = 8


def _routing_tables(route, n, tm):
    experts = route[:, :TOP_K].astype(jnp.int32).reshape(-1)
    onehot = (jnp.arange(N_EXPERTS, dtype=jnp.int32)[:, None] == experts[None]).astype(jnp.int32)
    running = jnp.cumsum(onehot, axis=1)
    rank = jnp.sum(running * onehot, axis=0) - 1
    count = running[:, -1]
    group = (count + tm - 1) // tm * tm
    group_end = jnp.cumsum(group)
    group_start = group_end - group
    dest = jnp.sum(onehot * group_start[:, None], axis=0) + rank
    n_tiles = TOP_K * n // tm + N_EXPERTS
    tile_start = jnp.arange(n_tiles, dtype=jnp.int32) * tm
    tile_expert = jnp.minimum(jnp.sum(tile_start[:, None] >= group_end[None], axis=1), N_EXPERTS - 1)
    n_used = (group_end[-1] // tm).reshape(1)
    return dest.astype(jnp.int32), tile_expert.astype(jnp.int32), n_used.astype(jnp.int32)


def _token_rows(index):
    return pl.ds(pl.multiple_of(index * TOKEN_TILE_ROWS, TOKEN_TILE_ROWS), TOKEN_TILE_ROWS)


def _dispatch_kernel(dest_ref, src_ref, init_ref, dst_ref, sem):
    del init_ref
    chunk = dest_ref.shape[2]

    def issue(token, carry):
        for k in range(TOP_K):
            pltpu.make_async_copy(src_ref.at[_token_rows(token)],
                                  dst_ref.at[_token_rows(dest_ref[0, 0, TOP_K * token + k])],
                                  sem).start(priority=k % 2)
        return carry

    lax.fori_loop(0, chunk // TOP_K, issue, 0, unroll=_DMA_ISSUE_UNROLL)
    for k in range(TOP_K):
        pltpu.make_async_copy(src_ref, dst_ref.at[pl.ds(0, src_ref.shape[0])], sem).wait()


def _dispatch(xn_tiles, dest, n_slots):
    n_assign = dest.shape[0]
    chunk = min(n_assign, _DISPATCH_CHUNK)
    steps = n_assign // chunk
    shape = jax.ShapeDtypeStruct((n_slots * TOKEN_TILE_ROWS, LANES), F32)
    return pl.pallas_call(
        _dispatch_kernel,
        grid=(steps,),
        in_specs=[pl.BlockSpec((1, 1, chunk), lambda i: (i, 0, 0), memory_space=pltpu.SMEM),
                  pl.BlockSpec((chunk // TOP_K * TOKEN_TILE_ROWS, LANES), lambda i: (i, 0)),
                  pl.BlockSpec(memory_space=pl.ANY)],
        out_specs=pl.BlockSpec(memory_space=pl.ANY),
        out_shape=shape,
        scratch_shapes=[pltpu.SemaphoreType.DMA(())],
        input_output_aliases={2: 0},
        compiler_params=_cparams("arbitrary"),
        name="moe_dispatch",
    )(dest.reshape(steps, 1, chunk), xn_tiles, jnp.zeros(shape.shape, F32))


def _expert_ffn_kernel(te_ref, nu_ref, x_ref, wg_ref, wu_ref, wd_ref, y_ref, xs_ref, acc_ref):
    del te_ref
    i = pl.program_id(0)
    j = pl.program_id(1)
    tm = xs_ref.shape[0]
    used = i < nu_ref[0]

    @pl.when(used & (j == 0))
    def _():
        for s in range(TOKEN_TILE_ROWS):
            xs_ref[:, s * LANES:(s + 1) * LANES] = _load_token_tiles(x_ref, tm, s).astype(BF16)

    @pl.when(used)
    def _():
        y = _swiglu_block(xs_ref[...], wg_ref[0], wu_ref[0], wd_ref[0])

        @pl.when(j == 0)
        def _():
            acc_ref[...] = y

        @pl.when(j > 0)
        def _():
            acc_ref[...] += y

    @pl.when(used & (j == _N_FF_BLOCKS - 1))
    def _():
        _store_token_tiles(y_ref, acc_ref[...])

    @pl.when(jnp.logical_not(used) & (j == _N_FF_BLOCKS - 1))
    def _():
        y_ref[...] = jnp.zeros(y_ref.shape, F32)


def _expert_ffn(x_sorted, tile_expert, n_used, wg16, wu16, wd16, tm):
    d = D_MODEL
    n_tiles = tile_expert.shape[0]
    rows = pl.BlockSpec((tm * TOKEN_TILE_ROWS, LANES), lambda i, j, te, nu: (i, 0))
    grid_spec = pltpu.PrefetchScalarGridSpec(
        num_scalar_prefetch=2,
        grid=(n_tiles, _N_FF_BLOCKS),
        in_specs=[rows,
                  pl.BlockSpec((1, d, _FF_BLOCK), lambda i, j, te, nu: (te[i], 0, j)),
                  pl.BlockSpec((1, d, _FF_BLOCK), lambda i, j, te, nu: (te[i], 0, j)),
                  pl.BlockSpec((1, _FF_BLOCK, d), lambda i, j, te, nu: (te[i], j, 0))],
        out_specs=rows,
        scratch_shapes=[pltpu.VMEM((tm, d), BF16), pltpu.VMEM((tm, d), F32)])
    return pl.pallas_call(
        _expert_ffn_kernel,
        grid_spec=grid_spec,
        out_shape=jax.ShapeDtypeStruct(x_sorted.shape, F32),
        compiler_params=_cparams("arbitrary", "arbitrary"),
        name="moe_expert_ffn",
    )(tile_expert, n_used, x_sorted, wg16, wu16, wd16)


def _combine_kernel(dest_ref, next_dest_ref, x1_ref, route_ref, y_ref, o_ref, buf_ref, sem):
    i = pl.program_id(0)
    last = pl.num_programs(0) - 1
    tc = x1_ref.shape[0]

    def copies(idx_ref, slot, t, k):
        return pltpu.make_async_copy(y_ref.at[_token_rows(idx_ref[0, 0, TOP_K * t + k])],
                                     buf_ref.at[slot, k, _token_rows(t)], sem.at[slot])

    def issue(idx_ref, slot):
        def body(t, carry):
            for k in range(TOP_K):
                copies(idx_ref, slot, t, k).start(priority=k % 2)
            return carry
        lax.fori_loop(0, tc, body, 0, unroll=_DMA_ISSUE_UNROLL)

    def finish(slot):
        for k in range(TOP_K):
            pltpu.make_async_copy(y_ref.at[pl.ds(0, tc * TOKEN_TILE_ROWS)], buf_ref.at[slot, k],
                                  sem.at[slot]).wait()
        route = route_ref[...]
        p1 = route[:, TOP_K:TOP_K + 1]
        p2 = route[:, TOP_K + 1:TOP_K + 2]
        for s in range(TOKEN_TILE_ROWS):
            sl = slice(s * LANES, (s + 1) * LANES)
            o_ref[:, sl] = (x1_ref[:, sl] + p1 * _load_token_tiles(buf_ref.at[slot, 0], tc, s)
                            + p2 * _load_token_tiles(buf_ref.at[slot, 1], tc, s))

    @pl.when(i == 0)
    def _():
        issue(dest_ref, 0)

    for slot in range(2):
        @pl.when(i % 2 == slot)
        def _(slot=slot):
            @pl.when(i < last)
            def _():
                issue(next_dest_ref, 1 - slot)
            finish(slot)


def _combine(x1, route, y_sorted, dest):
    n, d = x1.shape
    tc = min(n, _COMBINE_TOKENS)
    steps = n // tc
    dest3 = dest.reshape(steps, 1, TOP_K * tc)
    smem = lambda index_map: pl.BlockSpec((1, 1, TOP_K * tc), index_map, memory_space=pltpu.SMEM)
    return pl.pallas_call(
        _combine_kernel,
        grid=(steps,),
        in_specs=[smem(lambda i: (i, 0, 0)),
                  smem(lambda i: (jnp.minimum(i + 1, steps - 1), 0, 0)),
                  pl.BlockSpec((tc, d), lambda i: (i, 0)),
                  pl.BlockSpec((tc, LANES), lambda i: (i, 0)),
                  pl.BlockSpec(memory_space=pl.ANY)],
        out_specs=pl.BlockSpec((tc, d), lambda i: (i, 0)),
        out_shape=jax.ShapeDtypeStruct((n, d), F32),
        scratch_shapes=[pltpu.VMEM((2, TOP_K, tc * TOKEN_TILE_ROWS, LANES), F32),
                        pltpu.SemaphoreType.DMA((2,))],
        compiler_params=_cparams("arbitrary"),
        name="moe_combine",
    )(dest3, dest3, x1, route, y_sorted)


def _moe(xn_tiles, x1, route, wg16, wu16, wd16):
    n = x1.shape[0]
    tm = 512 if n >= 8192 else 256
    dest, tile_expert, n_used = _routing_tables(route, n, tm)
    x_sorted = _dispatch(xn_tiles, dest, tile_expert.shape[0] * tm)
    y_sorted = _expert_ffn(x_sorted, tile_expert, n_used, wg16, wu16, wd16, tm)
    return _combine(x1, route, y_sorted, dest)


def _mixer(x, layer_w, conv0, h0, s0, cache, batch, seq):
    keep = seq if cache is not None else min(BAND_PAST, seq)
    proj32, proj16, q16, k16, v16, k_tail, v_tail = _in_proj(
        x, layer_w["norm_mix"], layer_w["w_in"], layer_w["q_norm"], layer_w["k_norm"], seq, keep)
    ya, conv_new, h_new = _lru_branch(proj32, proj16, conv0, h0, layer_w["conv_w"], layer_w["conv_b"],
                                      layer_w["wri"], layer_w["lru_br"], layer_w["lru_bi"],
                                      layer_w["lru_lambda"], batch, seq)
    yb, s_new = _hgrn_branch(proj32, proj16, s0, layer_w["lb"], layer_w["hgrn_norm"], batch, seq)
    if cache is None:
        yc = _attention_prompt(q16, k16, v16, proj16, layer_w["rel_table"], batch, seq)
    else:
        yc = _attention_sample(q16, k16, v16, proj16, cache[0], cache[1], layer_w["rel_table"], batch,
                               seq)
    heads = (batch, keep, N_ATT_HEADS, ATT_HEAD_DIM)
    states = (conv_new, h_new.reshape(batch, D_MODEL), s_new, k_tail.reshape(heads), v_tail.reshape(heads))
    return ya, yb, yc, states


def kernel(x_prompt, x_sample, state_conv, state_lru, state_hgrn, cache_k, cache_v, norm_mix, w_in, conv_w, conv_b, lru_wr, lru_br, lru_wi, lru_bi, lru_lambda, hgrn_gamma, hgrn_norm, q_norm, k_norm, rel_bias_table, w_out, norm_ffn, ffn_w_gate, ffn_w_up, ffn_w_down, moe_router, moe_w_gate, moe_w_up, moe_w_down):
    depth = w_in.shape[0]
    bp, sp, d = x_prompt.shape
    bs, ss, _ = x_sample.shape

    p = jax.nn.softmax(hgrn_gamma.astype(F32), axis=0)
    lb_all = jnp.cumsum(p, axis=0) - p

    xp = x_prompt.reshape(bp * sp, d)
    xs = x_sample.reshape(bs * ss, d)
    zero_conv = jnp.zeros((bp, CONV_W - 1, d), F32)
    zero_h = jnp.zeros((bp, d), F32)
    zero_s = jnp.zeros((bp,) + state_hgrn.shape[2:], F32)

    st_p, st_s = [], []
    for l in range(depth):
        layer_w = dict(
            norm_mix=norm_mix[l], w_in=w_in[l].astype(BF16), conv_w=conv_w[l], conv_b=conv_b[l],
            wri=jnp.concatenate([lru_wr[l], lru_wi[l]], axis=-1).astype(BF16),
            lru_br=lru_br[l], lru_bi=lru_bi[l], lru_lambda=lru_lambda[l], lb=lb_all[l],
            hgrn_norm=hgrn_norm[l], q_norm=q_norm[l], k_norm=k_norm[l],
            rel_table=rel_bias_table[l])
        w_out16 = w_out[l].astype(BF16)
        cl = cache_k.shape[2]
        cache = (cache_k[l].reshape(bs * cl, d).astype(BF16),
                 cache_v[l].reshape(bs * cl, d).astype(BF16))
        dense = l % 2 == 0
        if dense:
            wg16 = ffn_w_gate[l // 2].astype(BF16)
            wu16 = ffn_w_up[l // 2].astype(BF16)
            wd16 = ffn_w_down[l // 2].astype(BF16)
            router = None
        else:
            wg16 = moe_w_gate[l // 2].astype(BF16)
            wu16 = moe_w_up[l // 2].astype(BF16)
            wd16 = moe_w_down[l // 2].astype(BF16)
            router = moe_router[l // 2]

        new_x = []
        for x, conv0, h0, s0, cch, batch, seq, acc in (
                (xp, zero_conv, zero_h, zero_s, None, bp, sp, st_p),
                (xs, state_conv[l], state_lru[l], state_hgrn[l], cache, bs, ss, st_s)):
            ya, yb, yc, states = _mixer(x, layer_w, conv0, h0, s0, cch, batch, seq)
            outs = _merge_out(ya, yb, yc, x, w_out16, norm_ffn[l], router)
            if router is None:
                new_x.append(_ffn(outs[1], outs[0], wg16, wu16, wd16))
            else:
                new_x.append(_moe(outs[1], outs[0], outs[2], wg16, wu16, wd16))
            acc.append(states)
        xp, xs = new_x

    stack = lambda sts, i: jnp.stack([s[i] for s in sts])
    return (xp.reshape(bp, sp, d), xs.reshape(bs, ss, d),
            stack(st_p, 0), stack(st_p, 1), stack(st_p, 2), stack(st_p, 3), stack(st_p, 4),
            stack(st_s, 0), stack(st_s, 1), stack(st_s, 2), stack(st_s, 3), stack(st_s, 4))
```
